```python
import math
import jax, jax.numpy as jnp
from jax import lax
import numpy as np

D_MODEL = 2048
BATCH = 8
SEQ = 4096
DEPTH = 1
DEC_BATCH = 1
DEC_SEQ = 8192
PAST_LEN = 128

HYENA_WIDTH = 1024
HYENA_ORDER = 2
SHORT_CONV = 3
FILT_EMB = 33
FILT_BANDS = (FILT_EMB - 1) // 2
FILT_HIDDEN = 64
DECAY_TARGET = 1e-2
FAST_DECAY_PCT = 0.3
SLOW_DECAY_PCT = 1.5
MAX_DECAY = math.log(DECAY_TARGET) / FAST_DECAY_PCT
MIN_DECAY = math.log(DECAY_TARGET) / SLOW_DECAY_PCT
N_HEADS = 8
N_KV_HEADS = 2
HEAD_REP = N_HEADS // N_KV_HEADS
HEAD_DIM = 128
ATTN_WIDTH = N_HEADS * HEAD_DIM
KV_WIDTH = N_KV_HEADS * HEAD_DIM
WINDOW = 128
BLOCK = 128
D_FF = 4 * D_MODEL
NORM_EPS = 1e-6
NEG_INF = -1e30
OFF_Q = HYENA_ORDER * HYENA_WIDTH + HYENA_WIDTH
OFF_K = OFF_Q + ATTN_WIDTH
OFF_V = OFF_K + KV_WIDTH
OFF_G = OFF_V + KV_WIDTH
D_IN = OFF_G + 2 * D_MODEL

kernel_name = "hyena_swa_gated_hybrid_encoder"


def rmsnorm(x, g):
    xf = x.astype(jnp.float32)
    y = xf * lax.rsqrt(jnp.mean(xf * xf, axis=-1, keepdims=True) + NORM_EPS)
    return (y * g.astype(jnp.float32)).astype(x.dtype)


def head_rmsnorm(x, g):
    return x * lax.rsqrt(jnp.mean(x * x, axis=-1, keepdims=True) + NORM_EPS) * g.astype(jnp.float32)


def short_conv(u, w, b):
    L = u.shape[1]
    up = jnp.pad(u, ((0, 0), (1, 1), (0, 0)))
    return up[:, :L] * w[0] + up[:, 1:L + 1] * w[1] + up[:, 2:L + 2] * w[2] + b


def hyena_filters(L, w1, b1, w2, b2, w3, freq):
    f32 = jnp.float32
    t = jnp.linspace(0.0, 1.0, L, dtype=f32)[:, None]
    w = 2.0 * math.pi * jnp.arange(L, dtype=f32)[:, None] / L
    f = jnp.linspace(1e-4, FILT_BANDS - 1, FILT_BANDS, dtype=f32)[None, :]
    fw = f * w
    z = jnp.concatenate([t, jnp.cos(fw), -jnp.sin(fw)], axis=-1)
    fr = freq.astype(f32)
    h = jnp.sin(fr * (z @ w1.astype(f32) + b1.astype(f32)))
    h = jnp.sin(fr * (h @ w2.astype(f32) + b2.astype(f32)))
    h = h @ w3.astype(f32)
    deltas = jnp.abs(jnp.linspace(MIN_DECAY, MAX_DECAY, HYENA_WIDTH, dtype=f32))
    decay = jnp.exp(-t * deltas[None, :])
    return h.reshape(L, HYENA_ORDER, 2, HYENA_WIDTH) * decay[:, None, None, :]


def bidir_long_conv(z, h_fwd, h_bwd, skip):
    L, C = h_fwd.shape
    k = jnp.concatenate([h_fwd, jnp.zeros((1, C), jnp.float32), h_bwd[:0:-1]], axis=0)
    zf = jnp.fft.rfft(z, n=2 * L, axis=1)
    kf = jnp.fft.rfft(k, n=2 * L, axis=0)
    y = jnp.fft.irfft(zf * kf[None], n=2 * L, axis=1)[:, :L]
    return y + z * skip.astype(jnp.float32)


def alibi_slopes():
    h = jnp.arange(1, N_HEADS + 1, dtype=jnp.float32)
    return jnp.exp2(-8.0 * h / N_HEADS)


def windowed_gqa(q, k, v, sink, q_g, k_g):
    B, L = q.shape[:2]
    nb = L // BLOCK
    q = head_rmsnorm(q.astype(jnp.float32), q_g)
    k = head_rmsnorm(k.astype(jnp.float32), k_g)
    v = v.astype(jnp.float32)
    qb = q.reshape(B, nb, BLOCK, N_KV_HEADS, HEAD_REP, HEAD_DIM)

    def band(a):
        ap = jnp.pad(a, ((0, 0), (BLOCK, BLOCK), (0, 0), (0, 0))).reshape(B, nb + 2, BLOCK, N_KV_HEADS, HEAD_DIM)
        return jnp.concatenate([ap[:, :-2], ap[:, 1:-1], ap[:, 2:]], axis=2)

    kb, vb = band(k), band(v)
    s = jnp.einsum('bnqgrd,bnkgd->bngrqk', qb, kb) * (HEAD_DIM ** -0.5)
    qpos = jnp.arange(L, dtype=jnp.int32).reshape(nb, BLOCK)
    kpos = (jnp.arange(nb, dtype=jnp.int32)[:, None] - 1) * BLOCK + jnp.arange(3 * BLOCK, dtype=jnp.int32)[None, :]
    dist = jnp.abs(qpos[:, :, None] - kpos[:, None, :])
    valid = (dist <= WINDOW) & (kpos >= 0)[:, None, :] & (kpos < L)[:, None, :]
    slopes = alibi_slopes().reshape(N_KV_HEADS, HEAD_REP)
    s = s - slopes[None, None, :, :, None, None] * dist.astype(jnp.float32)[None, :, None, None, :, :]
    s = jnp.where(valid[None, :, None, None, :, :], s, NEG_INF)
    sink_b = sink.astype(jnp.float32).reshape(N_KV_HEADS, HEAD_REP)[None, None, :, :, None, None]
    m = jnp.maximum(jnp.max(s, axis=-1, keepdims=True), sink_b)
    p = jnp.exp(s - m)
    denom = jnp.sum(p, axis=-1, keepdims=True) + jnp.exp(sink_b - m)
    o = jnp.einsum('bngrqk,bnkgd->bnqgrd', p / denom, vb)
    return o.reshape(B, L, ATTN_WIDTH)


def encoder_layer(x, norm_mix_g, w_in, conv_w, conv_b, filt_w1, filt_b1, filt_w2, filt_b2,
                  filt_w3, filt_freq, hyena_skip, q_norm_g, k_norm_g, attn_sink,
                  w_o_hyena, w_o_attn, w_out, norm_mlp_g, w_up, w_down):
    B, L, _ = x.shape
    dt = x.dtype
    xn = rmsnorm(x, norm_mix_g)
    proj = xn @ w_in

    u = short_conv(proj[..., :OFF_Q], conv_w, conv_b).astype(jnp.float32)
    z = u[..., :HYENA_WIDTH]
    filt = hyena_filters(L, filt_w1, filt_b1, filt_w2, filt_b2, filt_w3, filt_freq)
    for o in range(HYENA_ORDER):
        gate = u[..., (o + 1) * HYENA_WIDTH:(o + 2) * HYENA_WIDTH]
        z = gate * bidir_long_conv(z, filt[:, o, 0], filt[:, o, 1], hyena_skip[o])
    y_hy = z.astype(dt) @ w_o_hyena

    q = proj[..., OFF_Q:OFF_K].reshape(B, L, N_HEADS, HEAD_DIM)
    k = proj[..., OFF_K:OFF_V].reshape(B, L, N_KV_HEADS, HEAD_DIM)
    v = proj[..., OFF_V:OFF_G].reshape(B, L, N_KV_HEADS, HEAD_DIM)
    y_at = windowed_gqa(q, k, v, attn_sink, q_norm_g, k_norm_g).astype(dt) @ w_o_attn

    g = jax.nn.sigmoid(proj[..., OFF_G:].astype(jnp.float32))
    mixed = g[..., :D_MODEL] * y_hy.astype(jnp.float32) + g[..., D_MODEL:] * y_at.astype(jnp.float32)
    x = x + (mixed.astype(dt) @ w_out)

    hm = jnp.square(jax.nn.relu(rmsnorm(x, norm_mlp_g) @ w_up))
    return x + hm @ w_down


def setup_inputs(seed: int = 0) -> dict:
    key = jax.random.key(seed)
    ks = jax.random.split(key, 24)
    f32 = jnp.float32

    def nrm(k, shape, scale):
        return jax.random.normal(k, shape, f32) * scale

    return {
        "x_prompt": nrm(ks[0], (BATCH, SEQ, D_MODEL), 1.0),
        "x_sample": nrm(ks[1], (DEC_BATCH, DEC_SEQ, D_MODEL), 1.0),
        "norm_mix_g": 1.0 + nrm(ks[2], (DEPTH, D_MODEL), 0.02),
        "w_in": nrm(ks[3], (DEPTH, D_MODEL, D_IN), D_MODEL ** -0.5),
        "conv_w": nrm(ks[4], (DEPTH, SHORT_CONV, OFF_Q), SHORT_CONV ** -0.5),
        "conv_b": nrm(ks[5], (DEPTH, OFF_Q), 0.01),
        "filt_w1": nrm(ks[6], (DEPTH, FILT_EMB, FILT_HIDDEN), FILT_EMB ** -0.5),
        "filt_b1": nrm(ks[7], (DEPTH, FILT_HIDDEN), 0.1),
        "filt_w2": nrm(ks[8], (DEPTH, FILT_HIDDEN, FILT_HIDDEN), FILT_HIDDEN ** -0.5),
        "filt_b2": nrm(ks[9], (DEPTH, FILT_HIDDEN), 0.1),
        "filt_w3": nrm(ks[10], (DEPTH, FILT_HIDDEN, HYENA_ORDER * 2 * HYENA_WIDTH), 0.05 * FILT_HIDDEN ** -0.5),
        "filt_freq": 1.0 + nrm(ks[11], (DEPTH, FILT_HIDDEN), 0.1),
        "hyena_skip": nrm(ks[12], (DEPTH, HYENA_ORDER, HYENA_WIDTH), 0.5),
        "q_norm_g": 1.0 + nrm(ks[13], (DEPTH, HEAD_DIM), 0.02),
        "k_norm_g": 1.0 + nrm(ks[14], (DEPTH, HEAD_DIM), 0.02),
        "attn_sink": nrm(ks[15], (DEPTH, N_HEADS), 0.5),
        "w_o_hyena": nrm(ks[16], (DEPTH, HYENA_WIDTH, D_MODEL), HYENA_WIDTH ** -0.5),
        "w_o_attn": nrm(ks[17], (DEPTH, ATTN_WIDTH, D_MODEL), ATTN_WIDTH ** -0.5),
        "w_out": nrm(ks[18], (DEPTH, D_MODEL, D_MODEL), D_MODEL ** -0.5),
        "norm_mlp_g": 1.0 + nrm(ks[19], (DEPTH, D_MODEL), 0.02),
        "w_up": nrm(ks[20], (DEPTH, D_MODEL, D_FF), D_MODEL ** -0.5),
        "w_down": nrm(ks[21], (DEPTH, D_FF, D_MODEL), D_FF ** -0.5),
    }


def reference(x_prompt, x_sample, norm_mix_g, w_in, conv_w, conv_b, filt_w1, filt_b1, filt_w2,
              filt_b2, filt_w3, filt_freq, hyena_skip, q_norm_g, k_norm_g, attn_sink,
              w_o_hyena, w_o_attn, w_out, norm_mlp_g, w_up, w_down):
    y_prompt = x_prompt
    y_sample = x_sample
    for l in range(DEPTH):
        lp = (norm_mix_g[l], w_in[l], conv_w[l], conv_b[l], filt_w1[l], filt_b1[l], filt_w2[l],
              filt_b2[l], filt_w3[l], filt_freq[l], hyena_skip[l], q_norm_g[l], k_norm_g[l],
              attn_sink[l], w_o_hyena[l], w_o_attn[l], w_out[l], norm_mlp_g[l], w_up[l], w_down[l])
        y_prompt = encoder_layer(y_prompt, *lp)
        y_sample = encoder_layer(y_sample, *lp)
    return (y_prompt, y_sample)
```

```python
import functools
import math

import numpy as np
import jax
import jax.numpy as jnp
from jax import lax
from jax.experimental import pallas as pl
from jax.experimental.pallas import tpu as pltpu

F32 = jnp.float32
BF16 = jnp.bfloat16

NORM_EPS = 1e-6
NEG_INF = -1e30

HYENA_ORDER = 2
N_HEADS = 8
N_KV_HEADS = 2
HEAD_REP = N_HEADS // N_KV_HEADS
HEAD_DIM = 128
WINDOW = 128
FILT_BANDS = 16
DECAY_TARGET = 1e-2
MAX_DECAY = math.log(DECAY_TARGET) / 0.3
MIN_DECAY = math.log(DECAY_TARGET) / 1.5

FFT_N1 = 128
VMEM_LIMIT = 56 * 1024 * 1024


def _cparams(sem):
    return pltpu.CompilerParams(dimension_semantics=sem, vmem_limit_bytes=VMEM_LIMIT)


def _norm_matmul_kernel(x_ref, g_ref, w_ref, o_ref, xn_ref):
    @pl.when(pl.program_id(1) == 0)
    def _():
        x = x_ref[...]
        ms = jnp.mean(x * x, axis=-1, keepdims=True)
        xn_ref[...] = (x * lax.rsqrt(ms + NORM_EPS) * g_ref[...]).astype(BF16)

    o_ref[...] = jnp.dot(xn_ref[...], w_ref[...], preferred_element_type=F32)


def _norm_matmul(x, g, w, *, tm, tn):
    T, D = x.shape
    n_out = w.shape[1]
    return pl.pallas_call(
        _norm_matmul_kernel,
        grid=(T // tm, n_out // tn),
        in_specs=[
            pl.BlockSpec((tm, D), lambda i, j: (i, 0)),
            pl.BlockSpec((1, D), lambda i, j: (0, 0)),
            pl.BlockSpec((D, tn), lambda i, j: (0, j)),
        ],
        out_specs=pl.BlockSpec((tm, tn), lambda i, j: (i, j)),
        out_shape=jax.ShapeDtypeStruct((T, n_out), F32),
        scratch_shapes=[pltpu.VMEM((tm, D), BF16)],
        compiler_params=_cparams(("parallel", "arbitrary")),
    )(x, g.reshape(1, D), w)


def _short_conv_kernel(x_ref, xp_ref, xn_ref, w_ref, b_ref, o_ref, *, n_row_blocks):
    i = pl.program_id(1)
    x = x_ref[0]
    tm = x.shape[0]
    row = lax.broadcasted_iota(jnp.int32, x.shape, 0)
    prev_row = jnp.where(i > 0, xp_ref[0, 7:8, :], 0.0)
    next_row = jnp.where(i < n_row_blocks - 1, xn_ref[0, 0:1, :], 0.0)
    xm = jnp.where(row == 0, prev_row, pltpu.roll(x, 1, axis=0))
    xp = jnp.where(row == tm - 1, next_row, pltpu.roll(x, tm - 1, axis=0))
    o_ref[0, 0] = xm * w_ref[0:1, :] + x * w_ref[1:2, :] + xp * w_ref[2:3, :] + b_ref[...]


def _short_conv(proj, conv_w, conv_b, *, C, tm, tc):
    B, L, _ = proj.shape
    nrb = L // tm
    cpb = C // tc
    kern = functools.partial(_short_conv_kernel, n_row_blocks=nrb)
    return pl.pallas_call(
        kern,
        grid=(B, nrb, 3 * cpb),
        in_specs=[
            pl.BlockSpec((1, tm, tc), lambda b, i, j: (b, i, j)),
            pl.BlockSpec((1, 8, tc), lambda b, i, j: (b, jnp.maximum(i * (tm // 8) - 1, 0), j)),
            pl.BlockSpec((1, 8, tc), lambda b, i, j: (b, jnp.minimum((i + 1) * (tm // 8), L // 8 - 1), j)),
            pl.BlockSpec((3, tc), lambda b, i, j: (0, j)),
            pl.BlockSpec((1, tc), lambda b, i, j: (0, j)),
        ],
        out_specs=pl.BlockSpec((1, 1, tm, tc), lambda b, i, j: (j // cpb, b, i, j % cpb)),
        out_shape=jax.ShapeDtypeStruct((3, B, L, C), F32),
        compiler_params=_cparams(("parallel", "parallel", "parallel")),
    )(proj, proj, proj, conv_w, conv_b.reshape(1, -1))


def _filter_kernel(pos_ref, t_ref, w1_ref, b1_ref, w2_ref, b2_ref, w3_ref, fr_ref,
                   dl_ref, bw_ref, o_ref, h_ref):
    hi = lax.Precision.HIGHEST

    @pl.when(pl.program_id(1) == 0)
    def _():
        fr = fr_ref[...]
        a = jnp.dot(pos_ref[...], w1_ref[...], precision=hi, preferred_element_type=F32) + b1_ref[...]
        a = jnp.sin(fr * a)
        a = jnp.dot(a, w2_ref[...], precision=hi, preferred_element_type=F32) + b2_ref[...]
        h_ref[...] = jnp.sin(fr * a)

    h = jnp.dot(h_ref[...], w3_ref[...], precision=hi, preferred_element_type=F32)
    h = h * jnp.exp(-t_ref[...] * dl_ref[...])
    row = lax.broadcasted_iota(jnp.int32, h.shape, 0)
    first = (row == 0) & (pl.program_id(0) == 0)
    o_ref[...] = jnp.where(first & (bw_ref[...] > 0.0), 0.0, h)


def _filter_gen(L, C, w1, b1, w2, b2, w3, freq, *, tl, tc):
    emb = w1.shape[0]
    hid = w1.shape[1]
    n_f = w3.shape[1]
    t = np.linspace(0.0, 1.0, L, dtype=np.float32)[:, None]
    w = (2.0 * math.pi * np.arange(L, dtype=np.float32)[:, None] / L).astype(np.float32)
    f = np.linspace(1e-4, FILT_BANDS - 1, FILT_BANDS, dtype=np.float32)[None, :]
    fw = (f * w).astype(np.float32)
    pos = np.zeros((L, 128), np.float32)
    pos[:, :emb] = np.concatenate([t, np.cos(fw), -np.sin(fw)], axis=-1)
    deltas = np.abs(np.linspace(MIN_DECAY, MAX_DECAY, C, dtype=np.float32))
    dl = np.tile(deltas, n_f // C)[None, :]
    bw = np.tile(np.concatenate([np.zeros(C, np.float32), np.ones(C, np.float32)]), n_f // (2 * C))[None, :]
    w1p = jnp.zeros((128, hid), F32).at[:emb].set(w1)
    return pl.pallas_call(
        _filter_kernel,
        grid=(L // tl, n_f // tc),
        in_specs=[
            pl.BlockSpec((tl, 128), lambda i, j: (i, 0)),
            pl.BlockSpec((tl, 1), lambda i, j: (i, 0)),
            pl.BlockSpec((128, hid), lambda i, j: (0, 0)),
            pl.BlockSpec((1, hid), lambda i, j: (0, 0)),
            pl.BlockSpec((hid, hid), lambda i, j: (0, 0)),
            pl.BlockSpec((1, hid), lambda i, j: (0, 0)),
            pl.BlockSpec((hid, tc), lambda i, j: (0, j)),
            pl.BlockSpec((1, hid), lambda i, j: (0, 0)),
            pl.BlockSpec((1, tc), lambda i, j: (0, j)),
            pl.BlockSpec((1, tc), lambda i, j: (0, j)),
        ],
        out_specs=pl.BlockSpec((tl, tc), lambda i, j: (i, j)),
        out_shape=jax.ShapeDtypeStruct((L, n_f), F32),
        scratch_shapes=[pltpu.VMEM((tl, hid), F32)],
        compiler_params=_cparams(("parallel", "arbitrary")),
    )(jnp.asarray(pos), jnp.asarray(t), w1p, b1.reshape(1, hid), w2, b2.reshape(1, hid), w3,
      freq.reshape(1, hid), jnp.asarray(dl), jnp.asarray(bw))


@functools.lru_cache(maxsize=None)
def _dft_tables(L):
    N = 2 * L
    N1 = FFT_N1
    N2 = N // N1
    H2 = N2 // 2
    k2 = np.arange(H2, dtype=np.float64)
    n2 = np.arange(H2, dtype=np.float64)
    th = 2.0 * np.pi * np.outer(k2 + 0.5, n2) / N2
    ma = np.stack([np.cos(th), -np.sin(th)], axis=1).reshape(N2, H2)
    ga = np.stack([np.cos(th), -np.sin(th)], axis=1).reshape(N2, H2).T * (2.0 / N)
    k1 = np.arange(N1, dtype=np.float64)
    n1 = np.arange(N1, dtype=np.float64)
    kk = k1[None, :, None] * N2 + k2[:, None, None] + 0.5
    ph = 2.0 * np.pi * kk * n1[None, None, :] / N
    c, s = np.cos(ph), np.sin(ph)
    fwd = np.concatenate([np.concatenate([c, s], axis=2), np.concatenate([-s, c], axis=2)], axis=1)
    ct, st = np.swapaxes(c, 1, 2), np.swapaxes(s, 1, 2)
    inv = np.concatenate([np.concatenate([ct, -st], axis=2), np.concatenate([st, ct], axis=2)], axis=1)
    return tuple(a.astype(np.float32) for a in (ma, ga, fwd, inv))


def _mxu_tables(L):
    return tuple(jnp.asarray(a).astype(BF16) for a in _dft_tables(L))


def _stage_a_kernel(z_ref, m_ref, o_ref):
    o_ref[0] = jnp.dot(m_ref[...], z_ref[0, 0].astype(BF16), preferred_element_type=F32).astype(BF16)


def _stage_a(z4, comp, ma, *, tcol):
    _, B, H2, NC = z4.shape
    N2 = 2 * H2
    return pl.pallas_call(
        _stage_a_kernel,
        grid=(NC // tcol, B),
        in_specs=[
            pl.BlockSpec((1, 1, H2, tcol), lambda j, b: (comp, b, 0, j)),
            pl.BlockSpec((N2, H2), lambda j, b: (0, 0)),
        ],
        out_specs=pl.BlockSpec((1, N2, tcol), lambda j, b: (b, 0, j)),
        out_shape=jax.ShapeDtypeStruct((B, N2, NC), BF16),
        compiler_params=_cparams(("parallel", "parallel")),
    )(z4, ma)


def _stage_b_kernel(t_ref, f_ref, g_ref, k_ref, o_ref, *, kb):
    n1 = FFT_N1
    for i in range(kb):
        tc = t_ref.shape[-1]
        d = t_ref[0, i].reshape(2 * n1, tc)
        x = jnp.dot(f_ref[i], d, preferred_element_type=F32)
        xr, xi = x[:n1], x[n1:]
        kr, ki = k_ref[0, i, 0], k_ref[0, i, 1]
        y = jnp.concatenate([xr * kr - xi * ki, xr * ki + xi * kr], axis=0).astype(BF16)
        c = jnp.dot(g_ref[i], y, preferred_element_type=F32).astype(BF16)
        o_ref[0, i] = c.reshape(2, n1, tc)


def _stage_b(t1, fwd, inv, khat, order, *, kb, tc):
    B, H2, _, N1, C = t1.shape
    kern = functools.partial(_stage_b_kernel, kb=kb)
    return pl.pallas_call(
        kern,
        grid=(H2 // kb, C // tc, B),
        in_specs=[
            pl.BlockSpec((1, kb, 2, N1, tc), lambda k, c, b: (b, k, 0, 0, c)),
            pl.BlockSpec((kb, 2 * N1, 2 * N1), lambda k, c, b: (k, 0, 0)),
            pl.BlockSpec((kb, 2 * N1, 2 * N1), lambda k, c, b: (k, 0, 0)),
            pl.BlockSpec((1, kb, 2, N1, tc), lambda k, c, b: (order, k, 0, 0, c)),
        ],
        out_specs=pl.BlockSpec((1, kb, 2, N1, tc), lambda k, c, b: (b, k, 0, 0, c)),
        out_shape=jax.ShapeDtypeStruct(t1.shape, BF16),
        compiler_params=_cparams(("parallel", "parallel", "parallel")),
    )(t1, fwd, inv, khat)


def _filter_spectrum_kernel(tf_ref, tb_ref, f_ref, o_ref, *, kb):
    n1 = FFT_N1
    for i in range(kb):
        tc = tf_ref.shape[-1]
        xf = jnp.dot(f_ref[i], tf_ref[0, i].reshape(2 * n1, tc), preferred_element_type=F32)
        xb = jnp.dot(f_ref[i], tb_ref[0, i].reshape(2 * n1, tc), preferred_element_type=F32)
        o_ref[0, i, 0] = xf[:n1] + xb[:n1]
        o_ref[0, i, 1] = xf[n1:] - xb[n1:]


def _filter_spectrum(t1f, fwd, C, *, kb, tc):
    _, H2, _, N1, n_f = t1f.shape
    n_order = n_f // (2 * C)
    cpb = C // tc
    kern = functools.partial(_filter_spectrum_kernel, kb=kb)
    return pl.pallas_call(
        kern,
        grid=(n_order, H2 // kb, cpb),
        in_specs=[
            pl.BlockSpec((1, kb, 2, N1, tc), lambda o, k, c: (0, k, 0, 0, (2 * o) * cpb + c)),
            pl.BlockSpec((1, kb, 2, N1, tc), lambda o, k, c: (0, k, 0, 0, (2 * o + 1) * cpb + c)),
            pl.BlockSpec((kb, 2 * N1, 2 * N1), lambda o, k, c: (k, 0, 0)),
        ],
        out_specs=pl.BlockSpec((1, kb, 2, N1, tc), lambda o, k, c: (o, k, 0, 0, c)),
        out_shape=jax.ShapeDtypeStruct((n_order, H2, 2, N1, C), F32),
        compiler_params=_cparams(("parallel", "parallel", "parallel")),
    )(t1f, t1f, fwd)


def _stage_a_inv_kernel(t_ref, m_ref, z_ref, gate_ref, skip_ref, o_ref):
    y = jnp.dot(m_ref[...], t_ref[0], preferred_element_type=F32)
    o_ref[0, 0] = gate_ref[0, 0] * (y + skip_ref[...] * z_ref[0, 0])


def _stage_a_inv(t2, ga, z4, zcomp, gate4, gcomp, skip_row, *, tcol):
    B, N2, NC = t2.shape
    H2 = N2 // 2
    return pl.pallas_call(
        _stage_a_inv_kernel,
        grid=(NC // tcol, B),
        in_specs=[
            pl.BlockSpec((1, N2, tcol), lambda j, b: (b, 0, j)),
            pl.BlockSpec((H2, N2), lambda j, b: (0, 0)),
            pl.BlockSpec((1, 1, H2, tcol), lambda j, b: (zcomp, b, 0, j)),
            pl.BlockSpec((1, 1, H2, tcol), lambda j, b: (gcomp, b, 0, j)),
            pl.BlockSpec((1, tcol), lambda j, b: (0, 0)),
        ],
        out_specs=pl.BlockSpec((1, 1, H2, tcol), lambda j, b: (0, b, 0, j)),
        out_shape=jax.ShapeDtypeStruct((1, B, H2, NC), F32),
        compiler_params=_cparams(("parallel", "parallel")),
    )(t2, ga, z4, gate4, skip_row)


def _conv_tiles(L, C):
    N1 = FFT_N1
    H2 = L // N1
    tcol = min(8192, N1 * C)
    kb = min(8, H2)
    tc = min(512, C)
    return H2, tcol, kb, tc


def _filter_spectra(L, C, w1, b1, w2, b2, w3, freq):
    N1 = FFT_N1
    H2, tcol, kb, tc = _conv_tiles(L, C)
    ma, _, fwd, _ = _mxu_tables(L)
    n_f = w3.shape[1]
    h = _filter_gen(L, C, w1, b1, w2, b2, w3, freq, tl=min(512, L), tc=min(1024, n_f))
    t1f = _stage_a(h.reshape(1, 1, H2, N1 * n_f), 0, ma, tcol=tcol)
    return _filter_spectrum(t1f.reshape(1, H2, 2, N1, n_f), fwd, C, kb=kb, tc=tc)


def _hyena_mix(u, khat, skip):
    _, B, L, C = u.shape
    N1 = FFT_N1
    H2, tcol, kb, tc = _conv_tiles(L, C)
    ma, ga, fwd, inv = _mxu_tables(L)
    u4 = u.reshape(3, B, H2, N1 * C)
    z4, zc = u4, 0
    for o in range(HYENA_ORDER):
        t1 = _stage_a(z4, zc, ma, tcol=tcol)
        t2 = _stage_b(t1.reshape(B, H2, 2, N1, C), fwd, inv, khat, o, kb=kb, tc=tc)
        skip_row = jnp.tile(skip[o].astype(F32), tcol // C).reshape(1, tcol)
        z4 = _stage_a_inv(t2.reshape(B, 2 * H2, N1 * C), ga, z4, zc, u4, o + 1, skip_row, tcol=tcol)
        zc = 0
    return z4.reshape(B, L, C)


@functools.lru_cache(maxsize=None)
def _alibi_bias():
    q = np.arange(WINDOW)[:, None]
    kpos = np.arange(3 * WINDOW)[None, :] - WINDOW
    dist = np.abs(q - kpos).astype(np.float32)
    slopes = np.exp2(-8.0 * np.arange(1, N_HEADS + 1, dtype=np.float32) / N_HEADS)
    bias = -slopes[:, None, None] * dist[None]
    bias = np.where(dist[None] <= WINDOW, bias, NEG_INF).astype(np.float32)
    return bias


def _head_norm(x, g):
    return x * lax.rsqrt(jnp.mean(x * x, axis=-1, keepdims=True) + NORM_EPS) * g


def _attention_kernel(q_ref, kp_ref, kc_ref, kn_ref, vp_ref, vc_ref, vn_ref, qg_ref, kg_ref,
                      bias_ref, sink_ref, o_ref, *, n_blocks):
    i = pl.program_id(1)
    hd, blk, rep = HEAD_DIM, WINDOW, HEAD_REP
    q = q_ref[0]
    kk = jnp.concatenate([kp_ref[0], kc_ref[0], kn_ref[0]], axis=0)
    vv = jnp.concatenate([vp_ref[0], vc_ref[0], vn_ref[0]], axis=0)
    col = lax.broadcasted_iota(jnp.int32, (1, 3 * blk), 1)
    in_seq = ((col >= blk) | (i > 0)) & ((col < 2 * blk) | (i < n_blocks - 1))
    for g in range(N_KV_HEADS):
        kn = _head_norm(kk[:, g * hd:(g + 1) * hd], kg_ref[...]).astype(BF16)
        vg = vv[:, g * hd:(g + 1) * hd].astype(BF16)
        qs = jnp.concatenate([q[:, (g * rep + r) * hd:(g * rep + r + 1) * hd] for r in range(rep)], axis=0)
        qn = _head_norm(qs, qg_ref[...]).astype(BF16)
        s = lax.dot_general(qn, kn, (((1,), (1,)), ((), ())), preferred_element_type=F32)
        s = s.reshape(rep, blk, 3 * blk) * (hd ** -0.5) + bias_ref[g * rep:(g + 1) * rep]
        s = jnp.where(in_seq, s, NEG_INF)
        sink = sink_ref[g * rep:(g + 1) * rep]
        m = jnp.maximum(jnp.max(s, axis=-1, keepdims=True), sink)
        p = jnp.exp(s - m)
        denom = jnp.sum(p, axis=-1, keepdims=True) + jnp.exp(sink - m)
        pv = jnp.dot(p.reshape(rep * blk, 3 * blk).astype(BF16), vg, preferred_element_type=F32)
        o = pv.reshape(rep, blk, hd) / denom
        for r in range(rep):
            o_ref[0, :, (g * rep + r) * hd:(g * rep + r + 1) * hd] = o[r]


def _attention(proj, q_g, k_g, sink, *, off_q, off_k, off_v):
    B, L, _ = proj.shape
    blk = WINDOW
    nb = L // blk
    aw = N_HEADS * HEAD_DIM
    kw = N_KV_HEADS * HEAD_DIM
    qb, kcb, vcb = off_q // aw, off_k // kw, off_v // kw
    kern = functools.partial(_attention_kernel, n_blocks=nb)
    prev = lambda c: (lambda b, i: (b, jnp.maximum(i - 1, 0), c))
    cur = lambda c: (lambda b, i: (b, i, c))
    nxt = lambda c: (lambda b, i: (b, jnp.minimum(i + 1, nb - 1), c))
    return pl.pallas_call(
        kern,
        grid=(B, nb),
        in_specs=[
            pl.BlockSpec((1, blk, aw), cur(qb)),
            pl.BlockSpec((1, blk, kw), prev(kcb)),
            pl.BlockSpec((1, blk, kw), cur(kcb)),
            pl.BlockSpec((1, blk, kw), nxt(kcb)),
            pl.BlockSpec((1, blk, kw), prev(vcb)),
            pl.BlockSpec((1, blk, kw), cur(vcb)),
            pl.BlockSpec((1, blk, kw), nxt(vcb)),
            pl.BlockSpec((1, HEAD_DIM), lambda b, i: (0, 0)),
            pl.BlockSpec((1, HEAD_DIM), lambda b, i: (0, 0)),
            pl.BlockSpec((N_HEADS, blk, 3 * blk), lambda b, i: (0, 0, 0)),
            pl.BlockSpec((N_HEADS, 1, 1), lambda b, i: (0, 0, 0)),
        ],
        out_specs=pl.BlockSpec((1, blk, aw), lambda b, i: (b, i, 0)),
        out_shape=jax.ShapeDtypeStruct((B, L, aw), F32),
        compiler_params=_cparams(("parallel", "parallel")),
    )(proj, proj, proj, proj, proj, proj, proj, q_g.reshape(1, -1), k_g.reshape(1, -1),
      _alibi_bias(), sink.astype(F32).reshape(N_HEADS, 1, 1))


def _merge_kernel(z_ref, a_ref, g1_ref, g2_ref, x_ref, woh_ref, woa_ref, wout_ref, o_ref, zb_ref, ab_ref):
    @pl.when(pl.program_id(1) == 0)
    def _():
        o_ref[...] = x_ref[...]
        zb_ref[...] = z_ref[...].astype(BF16)
        ab_ref[...] = a_ref[...].astype(BF16)

    y_hy = jnp.dot(zb_ref[...], woh_ref[...], preferred_element_type=F32)
    y_at = jnp.dot(ab_ref[...], woa_ref[...], preferred_element_type=F32)
    mixed = jax.nn.sigmoid(g1_ref[...]) * y_hy + jax.nn.sigmoid(g2_ref[...]) * y_at
    o_ref[...] += jnp.dot(mixed.astype(BF16), wout_ref[...], preferred_element_type=F32)


def _merge(z, a, proj, x, w_oh, w_oa, w_out, *, off_g, tm, tj):
    T, D = x.shape
    cz, ca = z.shape[1], a.shape[1]
    g1b, g2b = off_g // tj, (off_g + D) // tj
    return pl.pallas_call(
        _merge_kernel,
        grid=(T // tm, D // tj),
        in_specs=[
            pl.BlockSpec((tm, cz), lambda i, j: (i, 0)),
            pl.BlockSpec((tm, ca), lambda i, j: (i, 0)),
            pl.BlockSpec((tm, tj), lambda i, j: (i, g1b + j)),
            pl.BlockSpec((tm, tj), lambda i, j: (i, g2b + j)),
            pl.BlockSpec((tm, D), lambda i, j: (i, 0)),
            pl.BlockSpec((cz, tj), lambda i, j: (0, j)),
            pl.BlockSpec((ca, tj), lambda i, j: (0, j)),
            pl.BlockSpec((tj, D), lambda i, j: (j, 0)),
        ],
        out_specs=pl.BlockSpec((tm, D), lambda i, j: (i, 0)),
        out_shape=jax.ShapeDtypeStruct((T, D), F32),
        scratch_shapes=[pltpu.VMEM((tm, cz), BF16), pltpu.VMEM((tm, ca), BF16)],
        compiler_params=_cparams(("parallel", "arbitrary")),
    )(z, a, proj, proj, x, w_oh, w_oa, w_out)


def _mlp_kernel(x_ref, g_ref, wu_ref, wd_ref, o_ref, xn_ref):
    @pl.when(pl.program_id(1) == 0)
    def _():
        x = x_ref[...]
        ms = jnp.mean(x * x, axis=-1, keepdims=True)
        xn_ref[...] = (x * lax.rsqrt(ms + NORM_EPS) * g_ref[...]).astype(BF16)
        o_ref[...] = x

    h = jnp.dot(xn_ref[...], wu_ref[...], preferred_element_type=F32)
    h = jnp.square(jnp.maximum(h, 0.0)).astype(BF16)
    o_ref[...] += jnp.dot(h, wd_ref[...], preferred_element_type=F32)


def _mlp(x, g, w_up, w_down, *, tm, tf):
    T, D = x.shape
    d_ff = w_up.shape[1]
    return pl.pallas_call(
        _mlp_kernel,
        grid=(T // tm, d_ff // tf),
        in_specs=[
            pl.BlockSpec((tm, D), lambda i, j: (i, 0)),
            pl.BlockSpec((1, D), lambda i, j: (0, 0)),
            pl.BlockSpec((D, tf), lambda i, j: (0, j)),
            pl.BlockSpec((tf, D), lambda i, j: (j, 0)),
        ],
        out_specs=pl.BlockSpec((tm, D), lambda i, j: (i, 0)),
        out_shape=jax.ShapeDtypeStruct((T, D), F32),
        scratch_shapes=[pltpu.VMEM((tm, D), BF16)],
        compiler_params=_cparams(("parallel", "arbitrary")),
    )(x, g.reshape(1, D), w_up, w_down)


def _encoder_layer(x, p, khat):
    B, L, D = x.shape
    C = p["hyena_skip"].shape[1]
    off_q = (HYENA_ORDER + 1) * C
    off_k = off_q + N_HEADS * HEAD_DIM
    off_v = off_k + N_KV_HEADS * HEAD_DIM
    off_g = off_v + N_KV_HEADS * HEAD_DIM
    x2 = x.reshape(B * L, D)
    proj = _norm_matmul(x2, p["norm_mix_g"], p["w_in"], tm=512, tn=512)
    proj3 = proj.reshape(B, L, -1)
    u = _short_conv(proj3, p["conv_w"], p["conv_b"], C=C, tm=512, tc=512)
    z = _hyena_mix(u, khat, p["hyena_skip"])
    att = _attention(proj3, p["q_norm_g"], p["k_norm_g"], p["attn_sink"], off_q=off_q, off_k=off_k, off_v=off_v)
    x1 = _merge(z.reshape(B * L, C), att.reshape(B * L, -1), proj, x2, p["w_o_hyena"], p["w_o_attn"],
                p["w_out"], off_g=off_g, tm=512, tj=512)
    y = _mlp(x1, p["norm_mlp_g"], p["w_up"], p["w_down"], tm=512, tf=512)
    return y.reshape(B, L, D)


def kernel(x_prompt, x_sample, norm_mix_g, w_in, conv_w, conv_b, filt_w1, filt_b1, filt_w2, filt_b2,
           filt_w3, filt_freq, hyena_skip, q_norm_g, k_norm_g, attn_sink, w_o_hyena, w_o_attn, w_out,
           norm_mlp_g, w_up, w_down):
    depth = w_in.shape[0]
    y_prompt, y_sample = x_prompt, x_sample
    for l in range(depth):
        p = dict(
            norm_mix_g=norm_mix_g[l], w_in=w_in[l].astype(BF16), conv_w=conv_w[l], conv_b=conv_b[l],
            hyena_skip=hyena_skip[l], q_norm_g=q_norm_g[l], k_norm_g=k_norm_g[l], attn_sink=attn_sink[l],
            w_o_hyena=w_o_hyena[l].astype(BF16), w_o_attn=w_o_attn[l].astype(BF16),
            w_out=w_out[l].astype(BF16), norm_mlp_g=norm_mlp_g[l],
            w_up=w_up[l].astype(BF16), w_down=w_down[l].astype(BF16),
        )
        C = hyena_skip.shape[2]
        filt = (filt_w1[l], filt_b1[l], filt_w2[l], filt_b2[l], filt_w3[l], filt_freq[l])
        outs = []
        for x in (y_prompt, y_sample):
            khat = _filter_spectra(x.shape[1], C, *filt)
            outs.append(_encoder_layer(x, p, khat))
        y_prompt, y_sample = outs
    return (y_prompt, y_sample)
```

```python
import functools
import math

import numpy as np
import jax
import jax.numpy as jnp
from jax import lax
from jax.experimental import pallas as pl
from jax.experimental.pallas import tpu as pltpu

F32 = jnp.float32
BF16 = jnp.bfloat16

NORM_EPS = 1e-6
NEG_INF = -1e30

HYENA_ORDER = 2
N_HEADS = 8
N_KV_HEADS = 2
HEAD_REP = N_HEADS // N_KV_HEADS
HEAD_DIM = 128
WINDOW = 128
FILT_BANDS = 16
DECAY_TARGET = 1e-2
MAX_DECAY = math.log(DECAY_TARGET) / 0.3
MIN_DECAY = math.log(DECAY_TARGET) / 1.5

FFT_N1 = 128
KRON_F32 = 8
KRON = 16
MAX_K2_PER_STEP = 32
K2_UNROLL = 4
VMEM_LIMIT = 56 * 1024 * 1024


def _cparams(sem):
    return pltpu.CompilerParams(dimension_semantics=sem, vmem_limit_bytes=VMEM_LIMIT)


def _norm_matmul_kernel(x_ref, g_ref, w_ref, o_ref, xn_ref):
    @pl.when(pl.program_id(1) == 0)
    def _():
        x = x_ref[...]
        ms = jnp.mean(x * x, axis=-1, keepdims=True)
        xn_ref[...] = (x * lax.rsqrt(ms + NORM_EPS) * g_ref[...]).astype(BF16)

    o_ref[...] = jnp.dot(xn_ref[...], w_ref[...], preferred_element_type=F32)


def _norm_matmul(x, g, w, *, tm, tn, name):
    T, D = x.shape
    n_out = w.shape[1]
    return pl.pallas_call(
        _norm_matmul_kernel,
        grid=(T // tm, n_out // tn),
        in_specs=[
            pl.BlockSpec((tm, D), lambda i, j: (i, 0)),
            pl.BlockSpec((1, D), lambda i, j: (0, 0)),
            pl.BlockSpec((D, tn), lambda i, j: (0, j)),
        ],
        out_specs=pl.BlockSpec((tm, tn), lambda i, j: (i, j)),
        out_shape=jax.ShapeDtypeStruct((T, n_out), F32),
        scratch_shapes=[pltpu.VMEM((tm, D), BF16)],
        compiler_params=_cparams(("parallel", "arbitrary")),
        name=name,
    )(x, g.reshape(1, D), w)


def _short_conv_kernel(x_ref, xp_ref, xn_ref, w_ref, b_ref, o_ref, *, n_row_blocks):
    i = pl.program_id(1)
    x = x_ref[0]
    tm = x.shape[0]
    row = lax.broadcasted_iota(jnp.int32, x.shape, 0)
    prev_row = jnp.where(i > 0, xp_ref[0, 7:8, :], 0.0)
    next_row = jnp.where(i < n_row_blocks - 1, xn_ref[0, 0:1, :], 0.0)
    xm = jnp.where(row == 0, prev_row, pltpu.roll(x, 1, axis=0))
    xp = jnp.where(row == tm - 1, next_row, pltpu.roll(x, tm - 1, axis=0))
    o_ref[0, 0] = xm * w_ref[0:1, :] + x * w_ref[1:2, :] + xp * w_ref[2:3, :] + b_ref[...]


def _short_conv(proj, conv_w, conv_b, *, C, tm, tc):
    B, L, _ = proj.shape
    nrb = L // tm
    cpb = C // tc
    kern = functools.partial(_short_conv_kernel, n_row_blocks=nrb)
    return pl.pallas_call(
        kern,
        grid=(B, nrb, 3 * cpb),
        in_specs=[
            pl.BlockSpec((1, tm, tc), lambda b, i, j: (b, i, j)),
            pl.BlockSpec((1, 8, tc), lambda b, i, j: (b, jnp.maximum(i * (tm // 8) - 1, 0), j)),
            pl.BlockSpec((1, 8, tc), lambda b, i, j: (b, jnp.minimum((i + 1) * (tm // 8), L // 8 - 1), j)),
            pl.BlockSpec((3, tc), lambda b, i, j: (0, j)),
            pl.BlockSpec((1, tc), lambda b, i, j: (0, j)),
        ],
        out_specs=pl.BlockSpec((1, 1, tm, tc), lambda b, i, j: (j // cpb, b, i, j % cpb)),
        out_shape=jax.ShapeDtypeStruct((3, B, L, C), F32),
        compiler_params=_cparams(("parallel", "parallel", "parallel")),
        name="short_conv",
    )(proj, proj, proj, conv_w, conv_b.reshape(1, -1))


def _filter_kernel(pos_ref, t_ref, w1_ref, b1_ref, w2_ref, b2_ref, w3_ref, fr_ref,
                   dl_ref, bw_ref, o_ref, h_ref):
    hi = lax.Precision.HIGHEST

    @pl.when(pl.program_id(1) == 0)
    def _():
        fr = fr_ref[...]
        a = jnp.dot(pos_ref[...], w1_ref[...], precision=hi, preferred_element_type=F32) + b1_ref[...]
        a = jnp.sin(fr * a)
        a = jnp.dot(a, w2_ref[...], precision=hi, preferred_element_type=F32) + b2_ref[...]
        h_ref[...] = jnp.sin(fr * a)

    h = jnp.dot(h_ref[...], w3_ref[...], precision=hi, preferred_element_type=F32)
    h = h * jnp.exp(-t_ref[...] * dl_ref[...])
    row = lax.broadcasted_iota(jnp.int32, h.shape, 0)
    first = (row == 0) & (pl.program_id(0) == 0)
    o_ref[...] = jnp.where(first & (bw_ref[...] > 0.0), 0.0, h)


def _filter_gen(L, C, w1, b1, w2, b2, w3, freq, *, tl, tc):
    emb = w1.shape[0]
    hid = w1.shape[1]
    n_f = w3.shape[1]
    t = np.linspace(0.0, 1.0, L, dtype=np.float32)[:, None]
    w = (2.0 * math.pi * np.arange(L, dtype=np.float32)[:, None] / L).astype(np.float32)
    f = np.linspace(1e-4, FILT_BANDS - 1, FILT_BANDS, dtype=np.float32)[None, :]
    fw = (f * w).astype(np.float32)
    pos = np.zeros((L, 128), np.float32)
    pos[:, :emb] = np.concatenate([t, np.cos(fw), -np.sin(fw)], axis=-1)
    deltas = np.abs(np.linspace(MIN_DECAY, MAX_DECAY, C, dtype=np.float32))
    dl = np.tile(deltas, n_f // C)[None, :]
    bw = np.tile(np.concatenate([np.zeros(C, np.float32), np.ones(C, np.float32)]), n_f // (2 * C))[None, :]
    w1p = jnp.zeros((128, hid), F32).at[:emb].set(w1)
    return pl.pallas_call(
        _filter_kernel,
        grid=(L // tl, n_f // tc),
        in_specs=[
            pl.BlockSpec((tl, 128), lambda i, j: (i, 0)),
            pl.BlockSpec((tl, 1), lambda i, j: (i, 0)),
            pl.BlockSpec((128, hid), lambda i, j: (0, 0)),
            pl.BlockSpec((1, hid), lambda i, j: (0, 0)),
            pl.BlockSpec((hid, hid), lambda i, j: (0, 0)),
            pl.BlockSpec((1, hid), lambda i, j: (0, 0)),
            pl.BlockSpec((hid, tc), lambda i, j: (0, j)),
            pl.BlockSpec((1, hid), lambda i, j: (0, 0)),
            pl.BlockSpec((1, tc), lambda i, j: (0, j)),
            pl.BlockSpec((1, tc), lambda i, j: (0, j)),
        ],
        out_specs=pl.BlockSpec((tl, tc), lambda i, j: (i, j)),
        out_shape=jax.ShapeDtypeStruct((L, n_f), F32),
        scratch_shapes=[pltpu.VMEM((tl, hid), F32)],
        compiler_params=_cparams(("parallel", "arbitrary")),
        name="filter_gen",
    )(jnp.asarray(pos), jnp.asarray(t), w1p, b1.reshape(1, hid), w2, b2.reshape(1, hid), w3,
      freq.reshape(1, hid), jnp.asarray(dl), jnp.asarray(bw))


@functools.lru_cache(maxsize=None)
def _dft_tables(L):
    N = 2 * L
    N1 = FFT_N1
    N2 = N // N1
    H2 = N2 // 2
    k2 = np.arange(H2, dtype=np.float64)
    n2 = np.arange(H2, dtype=np.float64)
    th = 2.0 * np.pi * np.outer(k2 + 0.5, n2) / N2
    ma = np.stack([np.cos(th), -np.sin(th)], axis=1).reshape(N2, H2)
    ga = ma.T * (2.0 / N)
    mak = np.kron(ma, np.eye(KRON_F32))
    gak = np.kron(ga, np.eye(KRON))
    k1 = np.arange(N1, dtype=np.float64)
    n1 = np.arange(N1, dtype=np.float64)
    kk = k1[None, :, None] * N2 + k2[:, None, None] + 0.5
    ph = 2.0 * np.pi * kk * n1[None, None, :] / N
    c, s = np.cos(ph), np.sin(ph)
    fwd = np.concatenate([np.concatenate([c, s], axis=2), np.concatenate([-s, c], axis=2)], axis=1)
    ct, st = np.swapaxes(c, 1, 2), np.swapaxes(s, 1, 2)
    inv = np.concatenate([np.concatenate([ct, -st], axis=2), np.concatenate([st, ct], axis=2)], axis=1)
    return tuple(a.astype(np.float32) for a in (mak, gak, fwd, inv))


def _conv_plan(L):
    H2 = L // FFT_N1
    kb = min(H2, MAX_K2_PER_STEP)
    tc = 256 if H2 <= MAX_K2_PER_STEP else 128
    return H2, kb, tc


def _mxu_tables(L):
    H2, kb, _ = _conv_plan(L)
    mak, gak, fwd, inv = (jnp.asarray(a).astype(BF16) for a in _dft_tables(L))
    return mak.reshape(H2 // kb, 2 * kb * KRON_F32, H2 * KRON_F32), gak, fwd, inv


def _stage_a(z_ref, ma_ref, a_ref):
    h2, tc = z_ref.shape[1], z_ref.shape[3]
    kf = KRON_F32
    for gp in range(FFT_N1 // KRON):
        parts = []
        for g in range(gp * (KRON // kf), (gp + 1) * (KRON // kf)):
            slab = z_ref[0, :, g * kf:(g + 1) * kf, :].reshape(h2 * kf, tc).astype(BF16)
            r = jnp.dot(ma_ref[0], slab, preferred_element_type=F32)
            parts.append(r.reshape(-1, kf, tc))
        a_ref[:, gp * KRON:(gp + 1) * KRON, :] = jnp.concatenate(parts, axis=1).astype(BF16)


def _conv_fwd_kernel(z_ref, ma_ref, f_ref, g_ref, k_ref, o_ref, a_ref, *, kb):
    n1 = FFT_N1
    tc = z_ref.shape[-1]
    _stage_a(z_ref.at[0], ma_ref, a_ref)

    def body(i, carry):
        d = a_ref[pl.ds(2 * i, 2)].reshape(2 * n1, tc)
        x = jnp.dot(f_ref[i], d, preferred_element_type=F32)
        xr, xi = x[:n1], x[n1:]
        kr, ki = k_ref[0, i, 0], k_ref[0, i, 1]
        y = jnp.concatenate([xr * kr - xi * ki, xr * ki + xi * kr], axis=0).astype(BF16)
        c = jnp.dot(g_ref[i], y, preferred_element_type=F32).astype(BF16)
        o_ref[0, i] = c.reshape(2, n1, tc)
        return carry

    lax.fori_loop(0, kb, body, 0, unroll=K2_UNROLL)


def _conv_fwd(z5, comp, mak, fwd, inv, khat, order):
    _, B, H2, N1, C = z5.shape
    _, kb, tc = _conv_plan(H2 * N1)
    kern = functools.partial(_conv_fwd_kernel, kb=kb)
    return pl.pallas_call(
        kern,
        grid=(C // tc, B, H2 // kb),
        in_specs=[
            pl.BlockSpec((1, 1, H2, N1, tc), lambda c, b, k: (comp, b, 0, 0, c)),
            pl.BlockSpec((1, 2 * kb * KRON_F32, H2 * KRON_F32), lambda c, b, k: (k, 0, 0)),
            pl.BlockSpec((kb, 2 * N1, 2 * N1), lambda c, b, k: (k, 0, 0)),
            pl.BlockSpec((kb, 2 * N1, 2 * N1), lambda c, b, k: (k, 0, 0)),
            pl.BlockSpec((1, kb, 2, N1, tc), lambda c, b, k: (order, k, 0, 0, c)),
        ],
        out_specs=pl.BlockSpec((1, kb, 2, N1, tc), lambda c, b, k: (b, k, 0, 0, c)),
        out_shape=jax.ShapeDtypeStruct((B, H2, 2, N1, C), BF16),
        scratch_shapes=[pltpu.VMEM((2 * kb, N1, tc), BF16)],
        compiler_params=_cparams(("parallel", "parallel", "arbitrary")),
        name="conv_fwd",
    )(z5, mak, fwd, inv, khat)


def _conv_inv_kernel(t_ref, ga_ref, z_ref, gate_ref, skip_ref, o_ref):
    h2, tc = z_ref.shape[2], z_ref.shape[4]
    for g in range(FFT_N1 // KRON):
        sl = slice(g * KRON, (g + 1) * KRON)
        slab = t_ref[0, :, :, sl, :].reshape(2 * h2 * KRON, tc)
        y = jnp.dot(ga_ref[...], slab, preferred_element_type=F32).reshape(h2, KRON, tc)
        out = gate_ref[0, 0, :, sl, :] * (y + skip_ref[...] * z_ref[0, 0, :, sl, :])
        o_ref[0, 0, :, sl, :] = out.astype(o_ref.dtype)


def _conv_inv(t2, gak, z5, zcomp, gate5, gcomp, skip_row, out_dtype):
    B, H2, _, N1, C = t2.shape
    _, _, tc = _conv_plan(H2 * N1)
    return pl.pallas_call(
        _conv_inv_kernel,
        grid=(C // tc, B),
        in_specs=[
            pl.BlockSpec((1, H2, 2, N1, tc), lambda c, b: (b, 0, 0, 0, c)),
            pl.BlockSpec((H2 * KRON, 2 * H2 * KRON), lambda c, b: (0, 0)),
            pl.BlockSpec((1, 1, H2, N1, tc), lambda c, b: (zcomp, b, 0, 0, c)),
            pl.BlockSpec((1, 1, H2, N1, tc), lambda c, b: (gcomp, b, 0, 0, c)),
            pl.BlockSpec((1, tc), lambda c, b: (0, c)),
        ],
        out_specs=pl.BlockSpec((1, 1, H2, N1, tc), lambda c, b: (0, b, 0, 0, c)),
        out_shape=jax.ShapeDtypeStruct((1, B, H2, N1, C), out_dtype),
        compiler_params=_cparams(("parallel", "parallel")),
        name="conv_inv",
    )(t2, gak, z5, gate5, skip_row)


def _filter_fwd_kernel(hf_ref, hb_ref, ma_ref, f_ref, o_ref, af_ref, ab_ref, *, kb):
    n1 = FFT_N1
    tc = hf_ref.shape[-1]
    _stage_a(hf_ref, ma_ref, af_ref)
    _stage_a(hb_ref, ma_ref, ab_ref)

    def body(i, carry):
        xf = jnp.dot(f_ref[i], af_ref[pl.ds(2 * i, 2)].reshape(2 * n1, tc), preferred_element_type=F32)
        xb = jnp.dot(f_ref[i], ab_ref[pl.ds(2 * i, 2)].reshape(2 * n1, tc), preferred_element_type=F32)
        o_ref[0, i, 0] = xf[:n1] + xb[:n1]
        o_ref[0, i, 1] = xf[n1:] - xb[n1:]
        return carry

    lax.fori_loop(0, kb, body, 0, unroll=K2_UNROLL)


def _filter_fwd(h4, mak, fwd, C):
    _, H2, N1, n_f = h4.shape
    _, kb, tc = _conv_plan(H2 * N1)
    n_order = n_f // (2 * C)
    cpb = C // tc
    kern = functools.partial(_filter_fwd_kernel, kb=kb)
    return pl.pallas_call(
        kern,
        grid=(n_order, cpb, H2 // kb),
        in_specs=[
            pl.BlockSpec((1, H2, N1, tc), lambda o, c, k: (0, 0, 0, (2 * o) * cpb + c)),
            pl.BlockSpec((1, H2, N1, tc), lambda o, c, k: (0, 0, 0, (2 * o + 1) * cpb + c)),
            pl.BlockSpec((1, 2 * kb * KRON_F32, H2 * KRON_F32), lambda o, c, k: (k, 0, 0)),
            pl.BlockSpec((kb, 2 * N1, 2 * N1), lambda o, c, k: (k, 0, 0)),
        ],
        out_specs=pl.BlockSpec((1, kb, 2, N1, tc), lambda o, c, k: (o, k, 0, 0, c)),
        out_shape=jax.ShapeDtypeStruct((n_order, H2, 2, N1, C), F32),
        scratch_shapes=[pltpu.VMEM((2 * kb, N1, tc), BF16), pltpu.VMEM((2 * kb, N1, tc), BF16)],
        compiler_params=_cparams(("parallel", "parallel", "arbitrary")),
        name="filter_fwd",
    )(h4, h4, mak, fwd)


def _filter_spectra(L, C, w1, b1, w2, b2, w3, freq):
    N1 = FFT_N1
    H2 = L // N1
    mak, _, fwd, _ = _mxu_tables(L)
    n_f = w3.shape[1]
    h = _filter_gen(L, C, w1, b1, w2, b2, w3, freq, tl=min(512, L), tc=min(1024, n_f))
    return _filter_fwd(h.reshape(1, H2, N1, n_f), mak, fwd, C)


def _hyena_mix(u, khat, skip):
    _, B, L, C = u.shape
    N1 = FFT_N1
    H2, _, tc = _conv_plan(L)
    mak, gak, fwd, inv = _mxu_tables(L)
    u5 = u.reshape(3, B, H2, N1, C)
    z5, zc = u5, 0
    for o in range(HYENA_ORDER):
        t2 = _conv_fwd(z5, zc, mak, fwd, inv, khat, o)
        out_dtype = BF16 if o == HYENA_ORDER - 1 else F32
        z5 = _conv_inv(t2, gak, z5, zc, u5, o + 1, skip[o].astype(F32).reshape(1, C), out_dtype)
        zc = 0
    return z5.reshape(B, L, C)


@functools.lru_cache(maxsize=None)
def _alibi_bias():
    q = np.arange(WINDOW)[:, None]
    kpos = np.arange(3 * WINDOW)[None, :] - WINDOW
    dist = np.abs(q - kpos).astype(np.float32)
    slopes = np.exp2(-8.0 * np.arange(1, N_HEADS + 1, dtype=np.float32) / N_HEADS)
    bias = -slopes[:, None, None] * dist[None]
    bias = np.where(dist[None] <= WINDOW, bias, NEG_INF).astype(np.float32)
    return bias


def _head_norm(x, g):
    return x * lax.rsqrt(jnp.mean(x * x, axis=-1, keepdims=True) + NORM_EPS) * g


def _attention_kernel(q_ref, kp_ref, kc_ref, kn_ref, vp_ref, vc_ref, vn_ref, qg_ref, kg_ref,
                      bias_ref, sink_ref, o_ref, *, n_blocks):
    i = pl.program_id(1)
    hd, blk, rep = HEAD_DIM, WINDOW, HEAD_REP
    q = q_ref[0]
    kk = jnp.concatenate([kp_ref[0], kc_ref[0], kn_ref[0]], axis=0)
    vv = jnp.concatenate([vp_ref[0], vc_ref[0], vn_ref[0]], axis=0)
    col = lax.broadcasted_iota(jnp.int32, (1, 3 * blk), 1)
    in_seq = ((col >= blk) | (i > 0)) & ((col < 2 * blk) | (i < n_blocks - 1))
    for g in range(N_KV_HEADS):
        kn = _head_norm(kk[:, g * hd:(g + 1) * hd], kg_ref[...]).astype(BF16)
        vg = vv[:, g * hd:(g + 1) * hd].astype(BF16)
        qs = jnp.concatenate([q[:, (g * rep + r) * hd:(g * rep + r + 1) * hd] for r in range(rep)], axis=0)
        qn = _head_norm(qs, qg_ref[...]).astype(BF16)
        s = lax.dot_general(qn, kn, (((1,), (1,)), ((), ())), preferred_element_type=F32)
        s = s.reshape(rep, blk, 3 * blk) * (hd ** -0.5) + bias_ref[g * rep:(g + 1) * rep]
        s = jnp.where(in_seq, s, NEG_INF)
        sink = sink_ref[g * rep:(g + 1) * rep]
        m = jnp.maximum(jnp.max(s, axis=-1, keepdims=True), sink)
        p = jnp.exp(s - m)
        denom = jnp.sum(p, axis=-1, keepdims=True) + jnp.exp(sink - m)
        pv = jnp.dot(p.reshape(rep * blk, 3 * blk).astype(BF16), vg, preferred_element_type=F32)
        o = pv.reshape(rep, blk, hd) / denom
        for r in range(rep):
            o_ref[0, :, (g * rep + r) * hd:(g * rep + r + 1) * hd] = o[r].astype(o_ref.dtype)


def _attention(proj, q_g, k_g, sink, *, off_q, off_k, off_v):
    B, L, _ = proj.shape
    blk = WINDOW
    nb = L // blk
    aw = N_HEADS * HEAD_DIM
    kw = N_KV_HEADS * HEAD_DIM
    qb, kcb, vcb = off_q // aw, off_k // kw, off_v // kw
    kern = functools.partial(_attention_kernel, n_blocks=nb)
    prev = lambda c: (lambda b, i: (b, jnp.maximum(i - 1, 0), c))
    cur = lambda c: (lambda b, i: (b, i, c))
    nxt = lambda c: (lambda b, i: (b, jnp.minimum(i + 1, nb - 1), c))
    return pl.pallas_call(
        kern,
        grid=(B, nb),
        in_specs=[
            pl.BlockSpec((1, blk, aw), cur(qb)),
            pl.BlockSpec((1, blk, kw), prev(kcb)),
            pl.BlockSpec((1, blk, kw), cur(kcb)),
            pl.BlockSpec((1, blk, kw), nxt(kcb)),
            pl.BlockSpec((1, blk, kw), prev(vcb)),
            pl.BlockSpec((1, blk, kw), cur(vcb)),
            pl.BlockSpec((1, blk, kw), nxt(vcb)),
            pl.BlockSpec((1, HEAD_DIM), lambda b, i: (0, 0)),
            pl.BlockSpec((1, HEAD_DIM), lambda b, i: (0, 0)),
            pl.BlockSpec((N_HEADS, blk, 3 * blk), lambda b, i: (0, 0, 0)),
            pl.BlockSpec((N_HEADS, 1, 1), lambda b, i: (0, 0, 0)),
        ],
        out_specs=pl.BlockSpec((1, blk, aw), lambda b, i: (b, i, 0)),
        out_shape=jax.ShapeDtypeStruct((B, L, aw), BF16),
        compiler_params=_cparams(("parallel", "parallel")),
        name="attention",
    )(proj, proj, proj, proj, proj, proj, proj, q_g.reshape(1, -1), k_g.reshape(1, -1),
      _alibi_bias(), sink.astype(F32).reshape(N_HEADS, 1, 1))


def _merge_kernel(z_ref, a_ref, g_ref, x_ref, woh_ref, woa_ref, wout_ref, o_ref, *, tj):
    d = x_ref.shape[1]
    z = z_ref[...]
    a = a_ref[...]
    acc = x_ref[...]
    for j in range(d // tj):
        cols = slice(j * tj, (j + 1) * tj)
        y_hy = jnp.dot(z, woh_ref[:, cols], preferred_element_type=F32)
        y_at = jnp.dot(a, woa_ref[:, cols], preferred_element_type=F32)
        g_hy = jax.nn.sigmoid(g_ref[:, cols])
        g_at = jax.nn.sigmoid(g_ref[:, d + j * tj:d + (j + 1) * tj])
        mixed = (g_hy * y_hy + g_at * y_at).astype(BF16)
        acc = acc + jnp.dot(mixed, wout_ref[cols, :], preferred_element_type=F32)
    o_ref[...] = acc


def _merge(z, a, gates, x, w_oh, w_oa, w_out, *, tm, tj):
    T, D = x.shape
    cz, ca = z.shape[1], a.shape[1]
    kern = functools.partial(_merge_kernel, tj=tj)
    return pl.pallas_call(
        kern,
        grid=(T // tm,),
        in_specs=[
            pl.BlockSpec((tm, cz), lambda i: (i, 0)),
            pl.BlockSpec((tm, ca), lambda i: (i, 0)),
            pl.BlockSpec((tm, 2 * D), lambda i: (i, 0)),
            pl.BlockSpec((tm, D), lambda i: (i, 0)),
            pl.BlockSpec((cz, D), lambda i: (0, 0)),
            pl.BlockSpec((ca, D), lambda i: (0, 0)),
            pl.BlockSpec((D, D), lambda i: (0, 0)),
        ],
        out_specs=pl.BlockSpec((tm, D), lambda i: (i, 0)),
        out_shape=jax.ShapeDtypeStruct((T, D), F32),
        compiler_params=_cparams(("parallel",)),
        name="merge",
    )(z, a, gates, x, w_oh, w_oa, w_out)


def _mlp_kernel(x_ref, g_ref, wu_ref, wd_ref, o_ref, xn_ref):
    @pl.when(pl.program_id(1) == 0)
    def _():
        x = x_ref[...]
        ms = jnp.mean(x * x, axis=-1, keepdims=True)
        xn_ref[...] = (x * lax.rsqrt(ms + NORM_EPS) * g_ref[...]).astype(BF16)
        o_ref[...] = x

    h = jnp.dot(xn_ref[...], wu_ref[...], preferred_element_type=F32)
    h = jnp.square(jnp.maximum(h, 0.0)).astype(BF16)
    o_ref[...] += jnp.dot(h, wd_ref[...], preferred_element_type=F32)


def _mlp(x, g, w_up, w_down, *, tm, tf):
    T, D = x.shape
    d_ff = w_up.shape[1]
    return pl.pallas_call(
        _mlp_kernel,
        grid=(T // tm, d_ff // tf),
        in_specs=[
            pl.BlockSpec((tm, D), lambda i, j: (i, 0)),
            pl.BlockSpec((1, D), lambda i, j: (0, 0)),
            pl.BlockSpec((D, tf), lambda i, j: (0, j)),
            pl.BlockSpec((tf, D), lambda i, j: (j, 0)),
        ],
        out_specs=pl.BlockSpec((tm, D), lambda i, j: (i, 0)),
        out_shape=jax.ShapeDtypeStruct((T, D), F32),
        scratch_shapes=[pltpu.VMEM((tm, D), BF16)],
        compiler_params=_cparams(("parallel", "arbitrary")),
        name="mlp",
    )(x, g.reshape(1, D), w_up, w_down)


def _row_tile(T, pref):
    return pref if T % pref == 0 else T


def _encoder_layer(x, p, khat):
    B, L, D = x.shape
    T = B * L
    C = p["hyena_skip"].shape[1]
    off_q = (HYENA_ORDER + 1) * C
    off_k = off_q + N_HEADS * HEAD_DIM
    off_v = off_k + N_KV_HEADS * HEAD_DIM
    x2 = x.reshape(T, D)
    tm = _row_tile(T, 1024)
    proj = _norm_matmul(x2, p["norm_mix_g"], p["w_in_main"], tm=tm, tn=512, name="in_proj_main")
    gates = _norm_matmul(x2, p["norm_mix_g"], p["w_in_gate"], tm=tm, tn=512, name="in_proj_gate")
    proj3 = proj.reshape(B, L, -1)
    u = _short_conv(proj3, p["conv_w"], p["conv_b"], C=C, tm=512, tc=512)
    z = _hyena_mix(u, khat, p["hyena_skip"])
    att = _attention(proj3, p["q_norm_g"], p["k_norm_g"], p["attn_sink"], off_q=off_q, off_k=off_k, off_v=off_v)
    x1 = _merge(z.reshape(T, C), att.reshape(T, -1), gates, x2, p["w_o_hyena"], p["w_o_attn"],
                p["w_out"], tm=_row_tile(T, 256), tj=512)
    y = _mlp(x1, p["norm_mlp_g"], p["w_up"], p["w_down"], tm=tm, tf=512)
    return y.reshape(B, L, D)


def kernel(x_prompt, x_sample, norm_mix_g, w_in, conv_w, conv_b, filt_w1, filt_b1, filt_w2, filt_b2,
           filt_w3, filt_freq, hyena_skip, q_norm_g, k_norm_g, attn_sink, w_o_hyena, w_o_attn, w_out,
           norm_mlp_g, w_up, w_down):
    depth = w_in.shape[0]
    D = x_prompt.shape[-1]
    C = hyena_skip.shape[2]
    off_g = w_in.shape[2] - 2 * D
    y_prompt, y_sample = x_prompt, x_sample
    for l in range(depth):
        p = dict(
            norm_mix_g=norm_mix_g[l], w_in_main=w_in[l, :, :off_g].astype(BF16),
            w_in_gate=w_in[l, :, off_g:].astype(BF16), conv_w=conv_w[l], conv_b=conv_b[l],
            hyena_skip=hyena_skip[l], q_norm_g=q_norm_g[l], k_norm_g=k_norm_g[l], attn_sink=attn_sink[l],
            w_o_hyena=w_o_hyena[l].astype(BF16), w_o_attn=w_o_attn[l].astype(BF16),
            w_out=w_out[l].astype(BF16), norm_mlp_g=norm_mlp_g[l],
            w_up=w_up[l].astype(BF16), w_down=w_down[l].astype(BF16),
        )
        filt = (filt_w1[l], filt_b1[l], filt_w2[l], filt_b2[l], filt_w3[l], filt_freq[l])
        outs = []
        for x in (y_prompt, y_sample):
            khat = _filter_spectra(x.shape[1], C, *filt)
            outs.append(_encoder_layer(x, p, khat))
        y_prompt, y_sample = outs
    return (y_prompt, y_sample)
```

```python
import functools
import math

import numpy as np
import jax
import jax.numpy as jnp
from jax import lax
from jax.experimental import pallas as pl
from jax.experimental.pallas import tpu as pltpu

F32 = jnp.float32
BF16 = jnp.bfloat16

NORM_EPS = 1e-6
NEG_INF = -1e30

HYENA_ORDER = 2
N_HEADS = 8
N_KV_HEADS = 2
HEAD_REP = N_HEADS // N_KV_HEADS
HEAD_DIM = 128
WINDOW = 128
ATT_Q_BLOCKS = 4
FILT_BANDS = 16
DECAY_TARGET = 1e-2
MAX_DECAY = math.log(DECAY_TARGET) / 0.3
MIN_DECAY = math.log(DECAY_TARGET) / 1.5

FFT_N1 = 128
MXU_LANES = 256
KRON_F32 = 8
KRON = 16
MAX_K2_PER_STEP = 32
K2_UNROLL = 4
VMEM_LIMIT = 56 * 1024 * 1024


def _cparams(sem):
    return pltpu.CompilerParams(dimension_semantics=sem, vmem_limit_bytes=VMEM_LIMIT)


def _norm_matmul_kernel(x_ref, g_ref, w_ref, o_ref, xn_ref):
    @pl.when(pl.program_id(1) == 0)
    def _():
        x = x_ref[...]
        ms = jnp.mean(x * x, axis=-1, keepdims=True)
        xn_ref[...] = (x * lax.rsqrt(ms + NORM_EPS) * g_ref[...]).astype(BF16)

    o_ref[...] = jnp.dot(xn_ref[...], w_ref[...], preferred_element_type=F32)


def _norm_matmul(x, g, w, *, tm, tn, name):
    T, D = x.shape
    n_out = w.shape[1]
    return pl.pallas_call(
        _norm_matmul_kernel,
        grid=(T // tm, n_out // tn),
        in_specs=[
            pl.BlockSpec((tm, D), lambda i, j: (i, 0)),
            pl.BlockSpec((1, D), lambda i, j: (0, 0)),
            pl.BlockSpec((D, tn), lambda i, j: (0, j)),
        ],
        out_specs=pl.BlockSpec((tm, tn), lambda i, j: (i, j)),
        out_shape=jax.ShapeDtypeStruct((T, n_out), F32),
        scratch_shapes=[pltpu.VMEM((tm, D), BF16)],
        compiler_params=_cparams(("parallel", "arbitrary")),
        name=name,
    )(x, g.reshape(1, D), w)


HALO = 16


def _rms_rows(x, g):
    ms = jnp.mean(x * x, axis=-1, keepdims=True)
    return (x * lax.rsqrt(ms + NORM_EPS) * g).astype(BF16)


def _in_proj_hyena_kernel(x_ref, xp_ref, xn_ref, g_ref, w_ref, cw_ref, cb_ref, o_ref, xs_ref, *, blocks_per_seq):
    tm = x_ref.shape[0]

    @pl.when(pl.program_id(1) == 0)
    def _():
        r = pl.program_id(0) % blocks_per_seq
        g = g_ref[...]
        xs_ref[0:HALO, :] = jnp.where(r > 0, _rms_rows(xp_ref[...], g), jnp.zeros((), BF16))
        xs_ref[HALO:HALO + tm, :] = _rms_rows(x_ref[...], g)
        xs_ref[HALO + tm:, :] = jnp.where(r < blocks_per_seq - 1, _rms_rows(xn_ref[...], g), jnp.zeros((), BF16))

    p = jnp.dot(xs_ref[...], w_ref[...], preferred_element_type=F32)
    rows = p.shape[0]
    prev = pltpu.roll(p, 1, axis=0)[HALO:HALO + tm]
    nxt = pltpu.roll(p, rows - 1, axis=0)[HALO:HALO + tm]
    o_ref[0, 0] = prev * cw_ref[0:1, :] + p[HALO:HALO + tm] * cw_ref[1:2, :] + nxt * cw_ref[2:3, :] + cb_ref[...]


def _in_proj_hyena(x, g, w, conv_w, conv_b, *, B, L, C, tm, tn):
    T, D = x.shape
    bps = L // tm
    cpb = C // tn
    hpb = tm // HALO
    kern = functools.partial(_in_proj_hyena_kernel, blocks_per_seq=bps)
    return pl.pallas_call(
        kern,
        grid=(T // tm, 3 * cpb),
        in_specs=[
            pl.BlockSpec((tm, D), lambda i, j: (i, 0)),
            pl.BlockSpec((HALO, D), lambda i, j: (jnp.maximum(i * hpb - 1, 0), 0)),
            pl.BlockSpec((HALO, D), lambda i, j: (jnp.minimum((i + 1) * hpb, T // HALO - 1), 0)),
            pl.BlockSpec((1, D), lambda i, j: (0, 0)),
            pl.BlockSpec((D, tn), lambda i, j: (0, j)),
            pl.BlockSpec((3, tn), lambda i, j: (0, j)),
            pl.BlockSpec((1, tn), lambda i, j: (0, j)),
        ],
        out_specs=pl.BlockSpec((1, 1, tm, tn), lambda i, j: (j // cpb, i // bps, i % bps, j % cpb)),
        out_shape=jax.ShapeDtypeStruct((3, B, L, C), F32),
        scratch_shapes=[pltpu.VMEM((tm + 2 * HALO, D), BF16)],
        compiler_params=_cparams(("parallel", "arbitrary")),
        name="in_proj_hyena",
    )(x, x, x, g.reshape(1, D), w, conv_w, conv_b.reshape(1, -1))


def _filter_kernel(pos_ref, t_ref, w1_ref, b1_ref, w2_ref, b2_ref, w3_ref, fr_ref,
                   dl_ref, bw_ref, o_ref, h_ref):
    hi = lax.Precision.HIGHEST

    @pl.when(pl.program_id(1) == 0)
    def _():
        fr = fr_ref[...]
        a = jnp.dot(pos_ref[...], w1_ref[...], precision=hi, preferred_element_type=F32) + b1_ref[...]
        a = jnp.sin(fr * a)
        a = jnp.dot(a, w2_ref[...], precision=hi, preferred_element_type=F32) + b2_ref[...]
        h_ref[...] = jnp.sin(fr * a).astype(BF16)

    h = jnp.dot(h_ref[...], w3_ref[...], preferred_element_type=F32)
    h = h * jnp.exp(-t_ref[...] * dl_ref[...])
    row = lax.broadcasted_iota(jnp.int32, h.shape, 0)
    first = (row == 0) & (pl.program_id(0) == 0)
    o_ref[...] = jnp.where(first & (bw_ref[...] > 0.0), 0.0, h)


def _filter_gen(L, C, w1, b1, w2, b2, w3, freq, *, tl, tc):
    emb = w1.shape[0]
    hid = w1.shape[1]
    n_f = w3.shape[1]
    t = np.linspace(0.0, 1.0, L, dtype=np.float32)[:, None]
    w = (2.0 * math.pi * np.arange(L, dtype=np.float32)[:, None] / L).astype(np.float32)
    f = np.linspace(1e-4, FILT_BANDS - 1, FILT_BANDS, dtype=np.float32)[None, :]
    fw = (f * w).astype(np.float32)
    pos = np.zeros((L, 128), np.float32)
    pos[:, :emb] = np.concatenate([t, np.cos(fw), -np.sin(fw)], axis=-1)
    deltas = np.abs(np.linspace(MIN_DECAY, MAX_DECAY, C, dtype=np.float32))
    dl = np.tile(deltas, n_f // C)[None, :]
    bw = np.tile(np.concatenate([np.zeros(C, np.float32), np.ones(C, np.float32)]), n_f // (2 * C))[None, :]
    w1p = jnp.zeros((128, hid), F32).at[:emb].set(w1)
    return pl.pallas_call(
        _filter_kernel,
        grid=(L // tl, n_f // tc),
        in_specs=[
            pl.BlockSpec((tl, 128), lambda i, j: (i, 0)),
            pl.BlockSpec((tl, 1), lambda i, j: (i, 0)),
            pl.BlockSpec((128, hid), lambda i, j: (0, 0)),
            pl.BlockSpec((1, hid), lambda i, j: (0, 0)),
            pl.BlockSpec((hid, hid), lambda i, j: (0, 0)),
            pl.BlockSpec((1, hid), lambda i, j: (0, 0)),
            pl.BlockSpec((hid, tc), lambda i, j: (0, j)),
            pl.BlockSpec((1, hid), lambda i, j: (0, 0)),
            pl.BlockSpec((1, tc), lambda i, j: (0, j)),
            pl.BlockSpec((1, tc), lambda i, j: (0, j)),
        ],
        out_specs=pl.BlockSpec((tl, tc), lambda i, j: (i, j)),
        out_shape=jax.ShapeDtypeStruct((L, n_f), F32),
        scratch_shapes=[pltpu.VMEM((tl, hid), BF16)],
        compiler_params=_cparams(("parallel", "arbitrary")),
        name="filter_gen",
    )(jnp.asarray(pos), jnp.asarray(t), w1p, b1.reshape(1, hid), w2, b2.reshape(1, hid), w3.astype(BF16),
      freq.reshape(1, hid), jnp.asarray(dl), jnp.asarray(bw))


@functools.lru_cache(maxsize=None)
def _dft_tables(L):
    N = 2 * L
    N1 = FFT_N1
    N2 = N // N1
    H2 = N2 // 2
    k2 = np.arange(H2, dtype=np.float64)
    n2 = np.arange(H2, dtype=np.float64)
    th = 2.0 * np.pi * np.outer(k2 + 0.5, n2) / N2
    ma = np.stack([np.cos(th), -np.sin(th)], axis=1).reshape(N2, H2)
    ga = ma.T * (2.0 / N)
    mak = np.kron(ma, np.eye(KRON_F32))
    gak = np.kron(ga, np.eye(KRON))
    k1 = np.arange(N1, dtype=np.float64)
    n1 = np.arange(N1, dtype=np.float64)
    kk = k1[None, :, None] * N2 + k2[:, None, None] + 0.5
    ph = 2.0 * np.pi * kk * n1[None, None, :] / N
    c, s = np.cos(ph), np.sin(ph)
    fwd = np.concatenate([np.concatenate([c, s], axis=2), np.concatenate([-s, c], axis=2)], axis=1)
    ct, st = np.swapaxes(c, 1, 2), np.swapaxes(s, 1, 2)
    inv = np.concatenate([np.concatenate([ct, -st], axis=2), np.concatenate([st, ct], axis=2)], axis=1)
    return tuple(a.astype(np.float32) for a in (mak, gak, fwd, inv))


def _conv_plan(L):
    H2 = L // FFT_N1
    long_seq = H2 > MAX_K2_PER_STEP
    kb = min(H2, MAX_K2_PER_STEP // 2 if long_seq else MAX_K2_PER_STEP)
    return H2, kb, MXU_LANES, 2 if long_seq else 1


def _mxu_tables(L):
    H2, kb, _, _ = _conv_plan(L)
    mak, gak, fwd, inv = (jnp.asarray(a).astype(BF16) for a in _dft_tables(L))
    return mak.reshape(H2 // kb, 2 * kb * KRON_F32, H2 * KRON_F32), gak, fwd, inv


def _stage_a(z_ref, ma_ref, a_ref):
    h2, tc = z_ref.shape[1], z_ref.shape[3]
    kf = KRON_F32
    for gp in range(FFT_N1 // KRON):
        parts = []
        for g in range(gp * (KRON // kf), (gp + 1) * (KRON // kf)):
            slab = z_ref[0, :, g * kf:(g + 1) * kf, :].reshape(h2 * kf, tc).astype(BF16)
            r = jnp.dot(ma_ref[0], slab, preferred_element_type=F32)
            parts.append(r.reshape(-1, kf, tc))
        a_ref[:, gp * KRON:(gp + 1) * KRON, :] = jnp.concatenate(parts, axis=1).astype(BF16)


def _conv_fwd_kernel(z_ref, ma_ref, f_ref, g_ref, k_ref, o_ref, a_ref, *, kb):
    n1 = FFT_N1
    tc = z_ref.shape[-1]
    _stage_a(z_ref.at[0], ma_ref, a_ref)

    def body(i, carry):
        d = a_ref[pl.ds(2 * i, 2)].reshape(2 * n1, tc)
        x = jnp.dot(f_ref[i], d, preferred_element_type=F32)
        xr, xi = x[:n1], x[n1:]
        kr, ki = k_ref[0, i, 0], k_ref[0, i, 1]
        y = jnp.concatenate([xr * kr - xi * ki, xr * ki + xi * kr], axis=0).astype(BF16)
        c = jnp.dot(g_ref[i], y, preferred_element_type=F32).astype(BF16)
        o_ref[0, i] = c.reshape(2, n1, tc)
        return carry

    lax.fori_loop(0, kb, body, 0, unroll=K2_UNROLL)


def _conv_fwd(z5, comp, mak, fwd, inv, khat, order):
    _, B, H2, N1, C = z5.shape
    _, kb, tc, _ = _conv_plan(H2 * N1)
    kern = functools.partial(_conv_fwd_kernel, kb=kb)
    return pl.pallas_call(
        kern,
        grid=(C // tc, B, H2 // kb),
        in_specs=[
            pl.BlockSpec((1, 1, H2, N1, tc), lambda c, b, k: (comp, b, 0, 0, c)),
            pl.BlockSpec((1, 2 * kb * KRON_F32, H2 * KRON_F32), lambda c, b, k: (k, 0, 0)),
            pl.BlockSpec((kb, 2 * N1, 2 * N1), lambda c, b, k: (k, 0, 0)),
            pl.BlockSpec((kb, 2 * N1, 2 * N1), lambda c, b, k: (k, 0, 0)),
            pl.BlockSpec((1, kb, 2, N1, tc), lambda c, b, k: (order, k, 0, 0, c)),
        ],
        out_specs=pl.BlockSpec((1, kb, 2, N1, tc), lambda c, b, k: (b, k, 0, 0, c)),
        out_shape=jax.ShapeDtypeStruct((B, H2, 2, N1, C), BF16),
        scratch_shapes=[pltpu.VMEM((2 * kb, N1, tc), BF16)],
        compiler_params=_cparams(("parallel", "parallel", "arbitrary")),
        name="conv_fwd",
    )(z5, mak, fwd, inv, khat)


def _conv_inv_kernel(t_ref, ga_ref, z_ref, gate_ref, skip_ref, o_ref):
    h2, rows, tc = z_ref.shape[2], z_ref.shape[3], z_ref.shape[4]
    for g in range(rows // KRON):
        sl = slice(g * KRON, (g + 1) * KRON)
        slab = t_ref[0, :, :, sl, :].reshape(2 * h2 * KRON, tc)
        y = jnp.dot(ga_ref[...], slab, preferred_element_type=F32).reshape(h2, KRON, tc)
        out = gate_ref[0, 0, :, sl, :] * (y + skip_ref[...] * z_ref[0, 0, :, sl, :])
        o_ref[0, 0, :, sl, :] = out.astype(o_ref.dtype)


def _conv_inv(t2, gak, z5, zcomp, gate5, gcomp, skip_row, out_dtype):
    B, H2, _, N1, C = t2.shape
    _, _, tc, ns = _conv_plan(H2 * N1)
    return pl.pallas_call(
        _conv_inv_kernel,
        grid=(C // tc, B, ns),
        in_specs=[
            pl.BlockSpec((1, H2, 2, N1 // ns, tc), lambda c, b, s: (b, 0, 0, s, c)),
            pl.BlockSpec((H2 * KRON, 2 * H2 * KRON), lambda c, b, s: (0, 0)),
            pl.BlockSpec((1, 1, H2, N1 // ns, tc), lambda c, b, s: (zcomp, b, 0, s, c)),
            pl.BlockSpec((1, 1, H2, N1 // ns, tc), lambda c, b, s: (gcomp, b, 0, s, c)),
            pl.BlockSpec((1, tc), lambda c, b, s: (0, c)),
        ],
        out_specs=pl.BlockSpec((1, 1, H2, N1 // ns, tc), lambda c, b, s: (0, b, 0, s, c)),
        out_shape=jax.ShapeDtypeStruct((1, B, H2, N1, C), out_dtype),
        compiler_params=_cparams(("parallel", "parallel", "parallel")),
        name="conv_inv",
    )(t2, gak, z5, gate5, skip_row)


def _filter_fwd_kernel(hf_ref, hb_ref, ma_ref, f_ref, o_ref, af_ref, ab_ref, *, kb):
    n1 = FFT_N1
    tc = hf_ref.shape[-1]
    _stage_a(hf_ref, ma_ref, af_ref)
    _stage_a(hb_ref, ma_ref, ab_ref)

    def body(i, carry):
        xf = jnp.dot(f_ref[i], af_ref[pl.ds(2 * i, 2)].reshape(2 * n1, tc), preferred_element_type=F32)
        xb = jnp.dot(f_ref[i], ab_ref[pl.ds(2 * i, 2)].reshape(2 * n1, tc), preferred_element_type=F32)
        o_ref[0, i, 0] = xf[:n1] + xb[:n1]
        o_ref[0, i, 1] = xf[n1:] - xb[n1:]
        return carry

    lax.fori_loop(0, kb, body, 0, unroll=K2_UNROLL)


def _filter_fwd(h4, mak, fwd, C):
    _, H2, N1, n_f = h4.shape
    _, kb, tc, _ = _conv_plan(H2 * N1)
    n_order = n_f // (2 * C)
    cpb = C // tc
    kern = functools.partial(_filter_fwd_kernel, kb=kb)
    return pl.pallas_call(
        kern,
        grid=(n_order, cpb, H2 // kb),
        in_specs=[
            pl.BlockSpec((1, H2, N1, tc), lambda o, c, k: (0, 0, 0, (2 * o) * cpb + c)),
            pl.BlockSpec((1, H2, N1, tc), lambda o, c, k: (0, 0, 0, (2 * o + 1) * cpb + c)),
            pl.BlockSpec((1, 2 * kb * KRON_F32, H2 * KRON_F32), lambda o, c, k: (k, 0, 0)),
            pl.BlockSpec((kb, 2 * N1, 2 * N1), lambda o, c, k: (k, 0, 0)),
        ],
        out_specs=pl.BlockSpec((1, kb, 2, N1, tc), lambda o, c, k: (o, k, 0, 0, c)),
        out_shape=jax.ShapeDtypeStruct((n_order, H2, 2, N1, C), F32),
        scratch_shapes=[pltpu.VMEM((2 * kb, N1, tc), BF16), pltpu.VMEM((2 * kb, N1, tc), BF16)],
        compiler_params=_cparams(("parallel", "parallel", "arbitrary")),
        name="filter_fwd",
    )(h4, h4, mak, fwd)


def _filter_spectra(L, C, w1, b1, w2, b2, w3, freq):
    N1 = FFT_N1
    H2 = L // N1
    mak, _, fwd, _ = _mxu_tables(L)
    n_f = w3.shape[1]
    h = _filter_gen(L, C, w1, b1, w2, b2, w3, freq, tl=min(512, L), tc=min(1024, n_f))
    return _filter_fwd(h.reshape(1, H2, N1, n_f), mak, fwd, C)


def _hyena_mix(u, khat, skip):
    _, B, L, C = u.shape
    N1 = FFT_N1
    H2 = L // N1
    mak, gak, fwd, inv = _mxu_tables(L)
    u5 = u.reshape(3, B, H2, N1, C)
    z5, zc = u5, 0
    for o in range(HYENA_ORDER):
        t2 = _conv_fwd(z5, zc, mak, fwd, inv, khat, o)
        out_dtype = BF16 if o == HYENA_ORDER - 1 else F32
        z5 = _conv_inv(t2, gak, z5, zc, u5, o + 1, skip[o].astype(F32).reshape(1, C), out_dtype)
        zc = 0
    return z5.reshape(B, L, C)


@functools.lru_cache(maxsize=None)
def _alibi_bias():
    q = np.arange(WINDOW)[:, None]
    kpos = np.arange(3 * WINDOW)[None, :] - WINDOW
    dist = np.abs(q - kpos).astype(np.float32)
    slopes = np.exp2(-8.0 * np.arange(1, N_HEADS + 1, dtype=np.float32) / N_HEADS)
    bias = -slopes[:, None, None] * dist[None]
    bias = np.where(dist[None] <= WINDOW, bias, NEG_INF).astype(np.float32)
    return bias


def _row_sumsq(x):
    sq = x * x
    hi = sq.astype(BF16)
    lo = (sq - hi.astype(F32)).astype(BF16)
    ones = jnp.ones((x.shape[-1], x.shape[-1]), BF16)
    return (jnp.dot(hi, ones, preferred_element_type=F32) + jnp.dot(lo, ones, preferred_element_type=F32))


def _head_norm(x, g):
    return x * lax.rsqrt(_row_sumsq(x) * (1.0 / x.shape[-1]) + NORM_EPS) * g


def _attention_kernel(q_ref, kp_ref, kc_ref, kn_ref, vp_ref, vc_ref, vn_ref, qg_ref, kg_ref,
                      bias_ref, sink_ref, o_ref, *, n_blocks, nq):
    i = pl.program_id(1)
    hd, blk, rep = HEAD_DIM, WINDOW, HEAD_REP
    kk = jnp.concatenate([kp_ref[0], kc_ref[0], kn_ref[0]], axis=0)
    vv = jnp.concatenate([vp_ref[0], vc_ref[0], vn_ref[0]], axis=0)
    col = lax.broadcasted_iota(jnp.int32, (1, 3 * blk), 1)
    for g in range(N_KV_HEADS):
        kn_all = _head_norm(kk[:, g * hd:(g + 1) * hd], kg_ref[...]).astype(BF16)
        vg_all = jnp.concatenate([vv[:, g * hd:(g + 1) * hd].astype(BF16), jnp.ones((kk.shape[0], hd), BF16)], axis=1)
        sink = sink_ref[g * rep:(g + 1) * rep]
        for t in range(nq):
            gi = i * nq + t
            in_seq = ((col >= blk) | (gi > 0)) & ((col < 2 * blk) | (gi < n_blocks - 1))
            rows = slice(t * blk, (t + 1) * blk)
            kn = kn_all[t * blk:(t + 3) * blk]
            vg = vg_all[t * blk:(t + 3) * blk]
            qs = jnp.concatenate(
                [q_ref[0, rows, (g * rep + r) * hd:(g * rep + r + 1) * hd] for r in range(rep)], axis=0)
            qn = _head_norm(qs, qg_ref[...]).astype(BF16)
            s = lax.dot_general(qn, kn, (((1,), (1,)), ((), ())), preferred_element_type=F32)
            s = s.reshape(rep, blk, 3 * blk) * (hd ** -0.5) + bias_ref[g * rep:(g + 1) * rep]
            s = jnp.where(in_seq, s, NEG_INF)
            m = jnp.maximum(jnp.max(s, axis=-1, keepdims=True), sink)
            p = jnp.exp(s - m)
            pv = jnp.dot(p.reshape(rep * blk, 3 * blk).astype(BF16), vg, preferred_element_type=F32)
            pv = pv.reshape(rep, blk, 2 * hd)
            denom = pv[:, :, hd:] + jnp.exp(sink - m)
            o = pv[:, :, :hd] / denom
            for r in range(rep):
                o_ref[0, rows, (g * rep + r) * hd:(g * rep + r + 1) * hd] = o[r].astype(o_ref.dtype)


def _attention(proj, q_g, k_g, sink, *, off_q, off_k, off_v):
    B, L, _ = proj.shape
    blk = WINDOW
    nb = L // blk
    nq = min(ATT_Q_BLOCKS, nb)
    aw = N_HEADS * HEAD_DIM
    kw = N_KV_HEADS * HEAD_DIM
    qb, kcb, vcb = off_q // aw, off_k // kw, off_v // kw
    kern = functools.partial(_attention_kernel, n_blocks=nb, nq=nq)
    prev = lambda c: (lambda b, i: (b, jnp.maximum(i * nq - 1, 0), c))
    cur = lambda c: (lambda b, i: (b, i, c))
    nxt = lambda c: (lambda b, i: (b, jnp.minimum((i + 1) * nq, nb - 1), c))
    return pl.pallas_call(
        kern,
        grid=(B, nb // nq),
        in_specs=[
            pl.BlockSpec((1, nq * blk, aw), cur(qb)),
            pl.BlockSpec((1, blk, kw), prev(kcb)),
            pl.BlockSpec((1, nq * blk, kw), cur(kcb)),
            pl.BlockSpec((1, blk, kw), nxt(kcb)),
            pl.BlockSpec((1, blk, kw), prev(vcb)),
            pl.BlockSpec((1, nq * blk, kw), cur(vcb)),
            pl.BlockSpec((1, blk, kw), nxt(vcb)),
            pl.BlockSpec((1, HEAD_DIM), lambda b, i: (0, 0)),
            pl.BlockSpec((1, HEAD_DIM), lambda b, i: (0, 0)),
            pl.BlockSpec((N_HEADS, blk, 3 * blk), lambda b, i: (0, 0, 0)),
            pl.BlockSpec((N_HEADS, 1, 1), lambda b, i: (0, 0, 0)),
        ],
        out_specs=pl.BlockSpec((1, nq * blk, aw), lambda b, i: (b, i, 0)),
        out_shape=jax.ShapeDtypeStruct((B, L, aw), BF16),
        compiler_params=_cparams(("parallel", "parallel")),
        name="attention",
    )(proj, proj, proj, proj, proj, proj, proj, q_g.reshape(1, -1), k_g.reshape(1, -1),
      _alibi_bias(), sink.astype(F32).reshape(N_HEADS, 1, 1))


def _merge_kernel(z_ref, a_ref, g_ref, x_ref, woh_ref, woa_ref, wout_ref, o_ref, *, tj):
    d = x_ref.shape[1]
    z = z_ref[...]
    a = a_ref[...]
    acc = x_ref[...]
    for j in range(d // tj):
        cols = slice(j * tj, (j + 1) * tj)
        y_hy = jnp.dot(z, woh_ref[:, cols], preferred_element_type=F32)
        y_at = jnp.dot(a, woa_ref[:, cols], preferred_element_type=F32)
        g_hy = jax.nn.sigmoid(g_ref[:, cols])
        g_at = jax.nn.sigmoid(g_ref[:, d + j * tj:d + (j + 1) * tj])
        mixed = (g_hy * y_hy + g_at * y_at).astype(BF16)
        acc = acc + jnp.dot(mixed, wout_ref[cols, :], preferred_element_type=F32)
    o_ref[...] = acc


def _merge(z, a, gates, x, w_oh, w_oa, w_out, *, tm, tj):
    T, D = x.shape
    cz, ca = z.shape[1], a.shape[1]
    kern = functools.partial(_merge_kernel, tj=tj)
    return pl.pallas_call(
        kern,
        grid=(T // tm,),
        in_specs=[
            pl.BlockSpec((tm, cz), lambda i: (i, 0)),
            pl.BlockSpec((tm, ca), lambda i: (i, 0)),
            pl.BlockSpec((tm, 2 * D), lambda i: (i, 0)),
            pl.BlockSpec((tm, D), lambda i: (i, 0)),
            pl.BlockSpec((cz, D), lambda i: (0, 0)),
            pl.BlockSpec((ca, D), lambda i: (0, 0)),
            pl.BlockSpec((D, D), lambda i: (0, 0)),
        ],
        out_specs=pl.BlockSpec((tm, D), lambda i: (i, 0)),
        out_shape=jax.ShapeDtypeStruct((T, D), F32),
        compiler_params=_cparams(("parallel",)),
        name="merge",
    )(z, a, gates, x, w_oh, w_oa, w_out)


def _mlp_kernel(x_ref, g_ref, wu_ref, wd_ref, o_ref, xn_ref):
    @pl.when(pl.program_id(1) == 0)
    def _():
        x = x_ref[...]
        ms = jnp.mean(x * x, axis=-1, keepdims=True)
        xn_ref[...] = (x * lax.rsqrt(ms + NORM_EPS) * g_ref[...]).astype(BF16)
        o_ref[...] = x

    h = jnp.dot(xn_ref[...], wu_ref[...], preferred_element_type=F32)
    h = jnp.square(jnp.maximum(h, 0.0)).astype(BF16)
    o_ref[...] += jnp.dot(h, wd_ref[...], preferred_element_type=F32)


def _mlp(x, g, w_up, w_down, *, tm, tf):
    T, D = x.shape
    d_ff = w_up.shape[1]
    return pl.pallas_call(
        _mlp_kernel,
        grid=(T // tm, d_ff // tf),
        in_specs=[
            pl.BlockSpec((tm, D), lambda i, j: (i, 0)),
            pl.BlockSpec((1, D), lambda i, j: (0, 0)),
            pl.BlockSpec((D, tf), lambda i, j: (0, j)),
            pl.BlockSpec((tf, D), lambda i, j: (j, 0)),
        ],
        out_specs=pl.BlockSpec((tm, D), lambda i, j: (i, 0)),
        out_shape=jax.ShapeDtypeStruct((T, D), F32),
        scratch_shapes=[pltpu.VMEM((tm, D), BF16)],
        compiler_params=_cparams(("parallel", "arbitrary")),
        name="mlp",
    )(x, g.reshape(1, D), w_up, w_down)


def _row_tile(T, pref):
    return pref if T % pref == 0 else T


def _encoder_layer(x, p, khat):
    B, L, D = x.shape
    T = B * L
    C = p["hyena_skip"].shape[1]
    off_q = 2 * D
    off_k = off_q + N_HEADS * HEAD_DIM
    off_v = off_k + N_KV_HEADS * HEAD_DIM
    x2 = x.reshape(T, D)
    tm = _row_tile(L, 1024)
    u = _in_proj_hyena(x2, p["norm_mix_g"], p["w_in_hyena"], p["conv_w"], p["conv_b"], B=B, L=L, C=C, tm=tm, tn=512)
    rest = _norm_matmul(x2, p["norm_mix_g"], p["w_in_rest"], tm=tm, tn=512, name="in_proj_rest")
    z = _hyena_mix(u, khat, p["hyena_skip"])
    att = _attention(rest.reshape(B, L, -1), p["q_norm_g"], p["k_norm_g"], p["attn_sink"],
                     off_q=off_q, off_k=off_k, off_v=off_v)
    x1 = _merge(z.reshape(T, C), att.reshape(T, -1), rest, x2, p["w_o_hyena"], p["w_o_attn"],
                p["w_out"], tm=_row_tile(T, 256), tj=512)
    y = _mlp(x1, p["norm_mlp_g"], p["w_up"], p["w_down"], tm=tm, tf=512)
    return y.reshape(B, L, D)


def kernel(x_prompt, x_sample, norm_mix_g, w_in, conv_w, conv_b, filt_w1, filt_b1, filt_w2, filt_b2,
           filt_w3, filt_freq, hyena_skip, q_norm_g, k_norm_g, attn_sink, w_o_hyena, w_o_attn, w_out,
           norm_mlp_g, w_up, w_down):
    depth = w_in.shape[0]
    D = x_prompt.shape[-1]
    C = hyena_skip.shape[2]
    off_g = w_in.shape[2] - 2 * D
    off_q = (HYENA_ORDER + 1) * C
    y_prompt, y_sample = x_prompt, x_sample
    for l in range(depth):
        p = dict(
            norm_mix_g=norm_mix_g[l], w_in_hyena=w_in[l, :, :off_q].astype(BF16),
            w_in_rest=jnp.concatenate([w_in[l, :, off_g:], w_in[l, :, off_q:off_g]], axis=1).astype(BF16),
            conv_w=conv_w[l], conv_b=conv_b[l],
            hyena_skip=hyena_skip[l], q_norm_g=q_norm_g[l], k_norm_g=k_norm_g[l], attn_sink=attn_sink[l],
            w_o_hyena=w_o_hyena[l].astype(BF16), w_o_attn=w_o_attn[l].astype(BF16),
            w_out=w_out[l].astype(BF16), norm_mlp_g=norm_mlp_g[l],
            w_up=w_up[l].astype(BF16), w_down=w_down[l].astype(BF16),
        )
        filt = (filt_w1[l], filt_b1[l], filt_w2[l], filt_b2[l], filt_w3[l], filt_freq[l])
        outs = []
        for x in (y_prompt, y_sample):
            khat = _filter_spectra(x.shape[1], C, *filt)
            outs.append(_encoder_layer(x, p, khat))
        y_prompt, y_sample = outs
    return (y_prompt, y_sample)
```

```python
import functools
import math

import numpy as np
import jax
import jax.numpy as jnp
from jax import lax
from jax.experimental import pallas as pl
from jax.experimental.pallas import tpu as pltpu

F32 = jnp.float32
BF16 = jnp.bfloat16

NORM_EPS = 1e-6
NEG_INF = -1e30

HYENA_ORDER = 2
N_HEADS = 8
N_KV_HEADS = 2
HEAD_REP = N_HEADS // N_KV_HEADS
HEAD_DIM = 128
WINDOW = 128
ATT_Q_BLOCKS = 4
FILT_BANDS = 16
DECAY_TARGET = 1e-2
MAX_DECAY = math.log(DECAY_TARGET) / 0.3
MIN_DECAY = math.log(DECAY_TARGET) / 1.5

FFT_N1 = 128
MXU_LANES = 256
KRON_F32 = 8
KRON = 16
MAX_K2_PER_STEP = 32
K2_UNROLL = 8
VMEM_LIMIT = 56 * 1024 * 1024


def _cparams(sem):
    return pltpu.CompilerParams(dimension_semantics=sem, vmem_limit_bytes=VMEM_LIMIT)


def _norm_matmul_kernel(x_ref, g_ref, w_ref, o_ref, xn_ref):
    @pl.when(pl.program_id(1) == 0)
    def _():
        x = x_ref[...]
        ms = jnp.mean(x * x, axis=-1, keepdims=True)
        xn_ref[...] = (x * lax.rsqrt(ms + NORM_EPS) * g_ref[...]).astype(BF16)

    o_ref[...] = jnp.dot(xn_ref[...], w_ref[...], preferred_element_type=F32).astype(o_ref.dtype)


def _norm_matmul(x, g, w, *, tm, tn, name, out_dtype):
    T, D = x.shape
    n_out = w.shape[1]
    return pl.pallas_call(
        _norm_matmul_kernel,
        grid=(T // tm, n_out // tn),
        in_specs=[
            pl.BlockSpec((tm, D), lambda i, j: (i, 0)),
            pl.BlockSpec((1, D), lambda i, j: (0, 0)),
            pl.BlockSpec((D, tn), lambda i, j: (0, j)),
        ],
        out_specs=pl.BlockSpec((tm, tn), lambda i, j: (i, j)),
        out_shape=jax.ShapeDtypeStruct((T, n_out), out_dtype),
        scratch_shapes=[pltpu.VMEM((tm, D), BF16)],
        compiler_params=_cparams(("parallel", "arbitrary")),
        name=name,
    )(x, g.reshape(1, D), w)


HALO = 16


def _rms_rows(x, g):
    ms = jnp.mean(x * x, axis=-1, keepdims=True)
    return (x * lax.rsqrt(ms + NORM_EPS) * g).astype(BF16)


def _in_proj_hyena_kernel(x_ref, xp_ref, xn_ref, g_ref, w_ref, cw_ref, cb_ref, o_ref, xs_ref, *, blocks_per_seq):
    tm = x_ref.shape[0]

    @pl.when(pl.program_id(1) == 0)
    def _():
        r = pl.program_id(0) % blocks_per_seq
        g = g_ref[...]
        xs_ref[0:HALO, :] = jnp.where(r > 0, _rms_rows(xp_ref[...], g), jnp.zeros((), BF16))
        xs_ref[HALO:HALO + tm, :] = _rms_rows(x_ref[...], g)
        xs_ref[HALO + tm:, :] = jnp.where(r < blocks_per_seq - 1, _rms_rows(xn_ref[...], g), jnp.zeros((), BF16))

    p = jnp.dot(xs_ref[...], w_ref[...], preferred_element_type=F32)
    rows = p.shape[0]
    prev = pltpu.roll(p, 1, axis=0)[HALO:HALO + tm]
    nxt = pltpu.roll(p, rows - 1, axis=0)[HALO:HALO + tm]
    u = prev * cw_ref[0:1, :] + p[HALO:HALO + tm] * cw_ref[1:2, :] + nxt * cw_ref[2:3, :] + cb_ref[...]
    o_ref[0, 0] = u.astype(o_ref.dtype)


def _in_proj_hyena(x, g, w, conv_w, conv_b, *, B, L, C, tm, tn):
    T, D = x.shape
    bps = L // tm
    cpb = C // tn
    hpb = tm // HALO
    kern = functools.partial(_in_proj_hyena_kernel, blocks_per_seq=bps)
    return pl.pallas_call(
        kern,
        grid=(T // tm, 3 * cpb),
        in_specs=[
            pl.BlockSpec((tm, D), lambda i, j: (i, 0)),
            pl.BlockSpec((HALO, D), lambda i, j: (jnp.maximum(i * hpb - 1, 0), 0)),
            pl.BlockSpec((HALO, D), lambda i, j: (jnp.minimum((i + 1) * hpb, T // HALO - 1), 0)),
            pl.BlockSpec((1, D), lambda i, j: (0, 0)),
            pl.BlockSpec((D, tn), lambda i, j: (0, j)),
            pl.BlockSpec((3, tn), lambda i, j: (0, j)),
            pl.BlockSpec((1, tn), lambda i, j: (0, j)),
        ],
        out_specs=pl.BlockSpec((1, 1, tm, tn), lambda i, j: (j // cpb, i // bps, i % bps, j % cpb)),
        out_shape=jax.ShapeDtypeStruct((3, B, L, C), BF16),
        scratch_shapes=[pltpu.VMEM((tm + 2 * HALO, D), BF16)],
        compiler_params=_cparams(("parallel", "arbitrary")),
        name="in_proj_hyena",
    )(x, x, x, g.reshape(1, D), w, conv_w, conv_b.reshape(1, -1))


def _filter_kernel(pos_ref, t_ref, w1_ref, b1_ref, w2_ref, b2_ref, w3_ref, fr_ref,
                   dl_ref, bw_ref, o_ref, h_ref):
    hi = lax.Precision.HIGHEST

    @pl.when(pl.program_id(1) == 0)
    def _():
        fr = fr_ref[...]
        a = jnp.dot(pos_ref[...], w1_ref[...], precision=hi, preferred_element_type=F32) + b1_ref[...]
        a = jnp.sin(fr * a)
        a = jnp.dot(a, w2_ref[...], precision=hi, preferred_element_type=F32) + b2_ref[...]
        h_ref[...] = jnp.sin(fr * a).astype(BF16)

    h = jnp.dot(h_ref[...], w3_ref[...], preferred_element_type=F32)
    h = h * jnp.exp(-t_ref[...] * dl_ref[...])
    row = lax.broadcasted_iota(jnp.int32, h.shape, 0)
    first = (row == 0) & (pl.program_id(0) == 0)
    o_ref[...] = jnp.where(first & (bw_ref[...] > 0.0), 0.0, h).astype(o_ref.dtype)


def _filter_gen(L, C, w1, b1, w2, b2, w3, freq, *, tl, tc):
    emb = w1.shape[0]
    hid = w1.shape[1]
    n_f = w3.shape[1]
    t = np.linspace(0.0, 1.0, L, dtype=np.float32)[:, None]
    w = (2.0 * math.pi * np.arange(L, dtype=np.float32)[:, None] / L).astype(np.float32)
    f = np.linspace(1e-4, FILT_BANDS - 1, FILT_BANDS, dtype=np.float32)[None, :]
    fw = (f * w).astype(np.float32)
    pos = np.zeros((L, 128), np.float32)
    pos[:, :emb] = np.concatenate([t, np.cos(fw), -np.sin(fw)], axis=-1)
    deltas = np.abs(np.linspace(MIN_DECAY, MAX_DECAY, C, dtype=np.float32))
    dl = np.tile(deltas, n_f // C)[None, :]
    bw = np.tile(np.concatenate([np.zeros(C, np.float32), np.ones(C, np.float32)]), n_f // (2 * C))[None, :]
    w1p = jnp.zeros((128, hid), F32).at[:emb].set(w1)
    return pl.pallas_call(
        _filter_kernel,
        grid=(L // tl, n_f // tc),
        in_specs=[
            pl.BlockSpec((tl, 128), lambda i, j: (i, 0)),
            pl.BlockSpec((tl, 1), lambda i, j: (i, 0)),
            pl.BlockSpec((128, hid), lambda i, j: (0, 0)),
            pl.BlockSpec((1, hid), lambda i, j: (0, 0)),
            pl.BlockSpec((hid, hid), lambda i, j: (0, 0)),
            pl.BlockSpec((1, hid), lambda i, j: (0, 0)),
            pl.BlockSpec((hid, tc), lambda i, j: (0, j)),
            pl.BlockSpec((1, hid), lambda i, j: (0, 0)),
            pl.BlockSpec((1, tc), lambda i, j: (0, j)),
            pl.BlockSpec((1, tc), lambda i, j: (0, j)),
        ],
        out_specs=pl.BlockSpec((tl, tc), lambda i, j: (i, j)),
        out_shape=jax.ShapeDtypeStruct((L, n_f), BF16),
        scratch_shapes=[pltpu.VMEM((tl, hid), BF16)],
        compiler_params=_cparams(("parallel", "arbitrary")),
        name="filter_gen",
    )(jnp.asarray(pos), jnp.asarray(t), w1p, b1.reshape(1, hid), w2, b2.reshape(1, hid), w3.astype(BF16),
      freq.reshape(1, hid), jnp.asarray(dl), jnp.asarray(bw))


@functools.lru_cache(maxsize=None)
def _dft_tables(L):
    N = 2 * L
    N1 = FFT_N1
    N2 = N // N1
    H2 = N2 // 2
    k2 = np.arange(H2, dtype=np.float64)
    n2 = np.arange(H2, dtype=np.float64)
    th = 2.0 * np.pi * np.outer(k2 + 0.5, n2) / N2
    ma = np.stack([np.cos(th), -np.sin(th)], axis=1).reshape(N2, H2)
    ga = ma.T * (2.0 / N)
    mak = np.kron(ma, np.eye(KRON_F32))
    gak = np.kron(ga, np.eye(KRON))
    k1 = np.arange(N1, dtype=np.float64)
    n1 = np.arange(N1, dtype=np.float64)
    kk = k1[None, :, None] * N2 + k2[:, None, None] + 0.5
    ph = 2.0 * np.pi * kk * n1[None, None, :] / N
    c, s = np.cos(ph), np.sin(ph)
    fwd = np.concatenate([np.concatenate([c, s], axis=2), np.concatenate([-s, c], axis=2)], axis=1)
    ct, st = np.swapaxes(c, 1, 2), np.swapaxes(s, 1, 2)
    inv = np.concatenate([np.concatenate([ct, -st], axis=2), np.concatenate([st, ct], axis=2)], axis=1)
    return tuple(a.astype(np.float32) for a in (mak, gak, fwd, inv))


def _conv_plan(L):
    H2 = L // FFT_N1
    long_seq = H2 > MAX_K2_PER_STEP
    kb = min(H2, MAX_K2_PER_STEP // 2 if long_seq else MAX_K2_PER_STEP)
    return H2, kb, MXU_LANES, 2 if long_seq else 1


def _mxu_tables(L):
    H2, kb, _, _ = _conv_plan(L)
    mak, gak, fwd, inv = (jnp.asarray(a).astype(BF16) for a in _dft_tables(L))
    return mak.reshape(H2 // kb, 2 * kb * KRON_F32, H2 * KRON_F32), gak, fwd, inv


def _stage_a(z_ref, ma_ref, a_ref):
    h2, tc = z_ref.shape[1], z_ref.shape[3]
    kf = KRON_F32
    for gp in range(FFT_N1 // KRON):
        rows16 = z_ref[0, :, gp * KRON:(gp + 1) * KRON, :].astype(F32)
        parts = []
        for g in range(KRON // kf):
            slab = rows16[:, g * kf:(g + 1) * kf, :].reshape(h2 * kf, tc).astype(BF16)
            r = jnp.dot(ma_ref[0], slab, preferred_element_type=F32)
            parts.append(r.reshape(-1, kf, tc))
        a_ref[:, gp * KRON:(gp + 1) * KRON, :] = jnp.concatenate(parts, axis=1).astype(BF16)


def _conv_fwd_kernel(z_ref, ma_ref, f_ref, g_ref, k_ref, o_ref, a_ref, *, kb):
    n1 = FFT_N1
    tc = z_ref.shape[-1]
    _stage_a(z_ref.at[0], ma_ref, a_ref)

    def body(i, carry):
        d = a_ref[pl.ds(2 * i, 2)].reshape(2 * n1, tc)
        x = jnp.dot(f_ref[i], d, preferred_element_type=F32)
        xr, xi = x[:n1], x[n1:]
        kr, ki = k_ref[0, i, 0], k_ref[0, i, 1]
        y = jnp.concatenate([xr * kr - xi * ki, xr * ki + xi * kr], axis=0).astype(BF16)
        c = jnp.dot(g_ref[i], y, preferred_element_type=F32).astype(BF16)
        o_ref[0, i] = c.reshape(2, n1, tc)
        return carry

    lax.fori_loop(0, kb, body, 0, unroll=K2_UNROLL)


def _conv_fwd(z5, comp, mak, fwd, inv, khat, order):
    _, B, H2, N1, C = z5.shape
    _, kb, tc, _ = _conv_plan(H2 * N1)
    kern = functools.partial(_conv_fwd_kernel, kb=kb)
    return pl.pallas_call(
        kern,
        grid=(C // tc, B, H2 // kb),
        in_specs=[
            pl.BlockSpec((1, 1, H2, N1, tc), lambda c, b, k: (comp, b, 0, 0, c)),
            pl.BlockSpec((1, 2 * kb * KRON_F32, H2 * KRON_F32), lambda c, b, k: (k, 0, 0)),
            pl.BlockSpec((kb, 2 * N1, 2 * N1), lambda c, b, k: (k, 0, 0)),
            pl.BlockSpec((kb, 2 * N1, 2 * N1), lambda c, b, k: (k, 0, 0)),
            pl.BlockSpec((1, kb, 2, N1, tc), lambda c, b, k: (order, k, 0, 0, c)),
        ],
        out_specs=pl.BlockSpec((1, kb, 2, N1, tc), lambda c, b, k: (b, k, 0, 0, c)),
        out_shape=jax.ShapeDtypeStruct((B, H2, 2, N1, C), BF16),
        scratch_shapes=[pltpu.VMEM((2 * kb, N1, tc), BF16)],
        compiler_params=_cparams(("parallel", "parallel", "arbitrary")),
        name="conv_fwd",
    )(z5, mak, fwd, inv, khat)


def _conv_inv_kernel(t_ref, ga_ref, z_ref, gate_ref, skip_ref, o_ref):
    h2, rows, tc = z_ref.shape[2], z_ref.shape[3], z_ref.shape[4]
    for g in range(rows // KRON):
        sl = slice(g * KRON, (g + 1) * KRON)
        slab = t_ref[0, :, :, sl, :].reshape(2 * h2 * KRON, tc)
        y = jnp.dot(ga_ref[...], slab, preferred_element_type=F32).reshape(h2, KRON, tc)
        out = gate_ref[0, 0, :, sl, :].astype(F32) * (y + skip_ref[...] * z_ref[0, 0, :, sl, :].astype(F32))
        o_ref[0, 0, :, sl, :] = out.astype(o_ref.dtype)


def _conv_inv(t2, gak, z5, zcomp, gate5, gcomp, skip_row, out_dtype):
    B, H2, _, N1, C = t2.shape
    _, _, tc, ns = _conv_plan(H2 * N1)
    return pl.pallas_call(
        _conv_inv_kernel,
        grid=(C // tc, B, ns),
        in_specs=[
            pl.BlockSpec((1, H2, 2, N1 // ns, tc), lambda c, b, s: (b, 0, 0, s, c)),
            pl.BlockSpec((H2 * KRON, 2 * H2 * KRON), lambda c, b, s: (0, 0)),
            pl.BlockSpec((1, 1, H2, N1 // ns, tc), lambda c, b, s: (zcomp, b, 0, s, c)),
            pl.BlockSpec((1, 1, H2, N1 // ns, tc), lambda c, b, s: (gcomp, b, 0, s, c)),
            pl.BlockSpec((1, tc), lambda c, b, s: (0, c)),
        ],
        out_specs=pl.BlockSpec((1, 1, H2, N1 // ns, tc), lambda c, b, s: (0, b, 0, s, c)),
        out_shape=jax.ShapeDtypeStruct((1, B, H2, N1, C), out_dtype),
        compiler_params=_cparams(("parallel", "parallel", "parallel")),
        name="conv_inv",
    )(t2, gak, z5, gate5, skip_row)


def _filter_fwd_kernel(hf_ref, hb_ref, ma_ref, f_ref, o_ref, af_ref, ab_ref, *, kb):
    n1 = FFT_N1
    tc = hf_ref.shape[-1]
    _stage_a(hf_ref, ma_ref, af_ref)
    _stage_a(hb_ref, ma_ref, ab_ref)

    def body(i, carry):
        xf = jnp.dot(f_ref[i], af_ref[pl.ds(2 * i, 2)].reshape(2 * n1, tc), preferred_element_type=F32)
        xb = jnp.dot(f_ref[i], ab_ref[pl.ds(2 * i, 2)].reshape(2 * n1, tc), preferred_element_type=F32)
        o_ref[0, i, 0] = xf[:n1] + xb[:n1]
        o_ref[0, i, 1] = xf[n1:] - xb[n1:]
        return carry

    lax.fori_loop(0, kb, body, 0, unroll=K2_UNROLL)


def _filter_fwd(h4, mak, fwd, C):
    _, H2, N1, n_f = h4.shape
    _, kb, tc, _ = _conv_plan(H2 * N1)
    n_order = n_f // (2 * C)
    cpb = C // tc
    kern = functools.partial(_filter_fwd_kernel, kb=kb)
    return pl.pallas_call(
        kern,
        grid=(n_order, cpb, H2 // kb),
        in_specs=[
            pl.BlockSpec((1, H2, N1, tc), lambda o, c, k: (0, 0, 0, (2 * o) * cpb + c)),
            pl.BlockSpec((1, H2, N1, tc), lambda o, c, k: (0, 0, 0, (2 * o + 1) * cpb + c)),
            pl.BlockSpec((1, 2 * kb * KRON_F32, H2 * KRON_F32), lambda o, c, k: (k, 0, 0)),
            pl.BlockSpec((kb, 2 * N1, 2 * N1), lambda o, c, k: (k, 0, 0)),
        ],
        out_specs=pl.BlockSpec((1, kb, 2, N1, tc), lambda o, c, k: (o, k, 0, 0, c)),
        out_shape=jax.ShapeDtypeStruct((n_order, H2, 2, N1, C), F32),
        scratch_shapes=[pltpu.VMEM((2 * kb, N1, tc), BF16), pltpu.VMEM((2 * kb, N1, tc), BF16)],
        compiler_params=_cparams(("parallel", "parallel", "arbitrary")),
        name="filter_fwd",
    )(h4, h4, mak, fwd)


def _filter_spectra(L, C, w1, b1, w2, b2, w3, freq):
    N1 = FFT_N1
    H2 = L // N1
    mak, _, fwd, _ = _mxu_tables(L)
    n_f = w3.shape[1]
    h = _filter_gen(L, C, w1, b1, w2, b2, w3, freq, tl=min(512, L), tc=min(1024, n_f))
    return _filter_fwd(h.reshape(1, H2, N1, n_f), mak, fwd, C)


def _hyena_mix(u, khat, skip):
    _, B, L, C = u.shape
    N1 = FFT_N1
    H2 = L // N1
    mak, gak, fwd, inv = _mxu_tables(L)
    u5 = u.reshape(3, B, H2, N1, C)
    z5, zc = u5, 0
    for o in range(HYENA_ORDER):
        t2 = _conv_fwd(z5, zc, mak, fwd, inv, khat, o)
        z5 = _conv_inv(t2, gak, z5, zc, u5, o + 1, skip[o].astype(F32).reshape(1, C), BF16)
        zc = 0
    return z5.reshape(B, L, C)


@functools.lru_cache(maxsize=None)
def _alibi_bias():
    q = np.arange(WINDOW)[:, None]
    kpos = np.arange(3 * WINDOW)[None, :] - WINDOW
    dist = np.abs(q - kpos).astype(np.float32)
    slopes = np.exp2(-8.0 * np.arange(1, N_HEADS + 1, dtype=np.float32) / N_HEADS)
    bias = -slopes[:, None, None] * dist[None]
    bias = np.where(dist[None] <= WINDOW, bias, NEG_INF).astype(np.float32)
    return bias


def _row_sumsq(x):
    sq = x * x
    hi = sq.astype(BF16)
    lo = (sq - hi.astype(F32)).astype(BF16)
    ones = jnp.ones((x.shape[-1], x.shape[-1]), BF16)
    return (jnp.dot(hi, ones, preferred_element_type=F32) + jnp.dot(lo, ones, preferred_element_type=F32))


def _head_norm(x, g):
    return x * lax.rsqrt(_row_sumsq(x) * (1.0 / x.shape[-1]) + NORM_EPS) * g


def _attention_kernel(q_ref, kp_ref, kc_ref, kn_ref, vp_ref, vc_ref, vn_ref, qg_ref, kg_ref,
                      bias_ref, sink_ref, o_ref, *, n_blocks, nq):
    i = pl.program_id(1)
    hd, blk, rep = HEAD_DIM, WINDOW, HEAD_REP
    kk = jnp.concatenate([kp_ref[0], kc_ref[0], kn_ref[0]], axis=0)
    vv = jnp.concatenate([vp_ref[0], vc_ref[0], vn_ref[0]], axis=0)
    col = lax.broadcasted_iota(jnp.int32, (1, 3 * blk), 1)
    for g in range(N_KV_HEADS):
        kn_all = _head_norm(kk[:, g * hd:(g + 1) * hd].astype(F32), kg_ref[...]).astype(BF16)
        vg_all = jnp.concatenate([vv[:, g * hd:(g + 1) * hd].astype(BF16), jnp.ones((kk.shape[0], hd), BF16)], axis=1)
        sink = sink_ref[g * rep:(g + 1) * rep]
        for t in range(nq):
            gi = i * nq + t
            in_seq = ((col >= blk) | (gi > 0)) & ((col < 2 * blk) | (gi < n_blocks - 1))
            rows = slice(t * blk, (t + 1) * blk)
            kn = kn_all[t * blk:(t + 3) * blk]
            vg = vg_all[t * blk:(t + 3) * blk]
            qs = jnp.concatenate(
                [q_ref[0, rows, (g * rep + r) * hd:(g * rep + r + 1) * hd] for r in range(rep)], axis=0)
            qn = _head_norm(qs.astype(F32), qg_ref[...]).astype(BF16)
            s = lax.dot_general(qn, kn, (((1,), (1,)), ((), ())), preferred_element_type=F32)
            s = s.reshape(rep, blk, 3 * blk) * (hd ** -0.5) + bias_ref[g * rep:(g + 1) * rep]
            s = jnp.where(in_seq, s, NEG_INF)
            m = jnp.maximum(jnp.max(s, axis=-1, keepdims=True), sink)
            p = jnp.exp(s - m)
            pv = jnp.dot(p.reshape(rep * blk, 3 * blk).astype(BF16), vg, preferred_element_type=F32)
            pv = pv.reshape(rep, blk, 2 * hd)
            denom = pv[:, :, hd:] + jnp.exp(sink - m)
            o = pv[:, :, :hd] / denom
            for r in range(rep):
                o_ref[0, rows, (g * rep + r) * hd:(g * rep + r + 1) * hd] = o[r].astype(o_ref.dtype)


def _attention(proj, q_g, k_g, sink, *, off_q, off_k, off_v):
    B, L, _ = proj.shape
    blk = WINDOW
    nb = L // blk
    nq = min(ATT_Q_BLOCKS, nb)
    aw = N_HEADS * HEAD_DIM
    kw = N_KV_HEADS * HEAD_DIM
    qb, kcb, vcb = off_q // aw, off_k // kw, off_v // kw
    kern = functools.partial(_attention_kernel, n_blocks=nb, nq=nq)
    prev = lambda c: (lambda b, i: (b, jnp.maximum(i * nq - 1, 0), c))
    cur = lambda c: (lambda b, i: (b, i, c))
    nxt = lambda c: (lambda b, i: (b, jnp.minimum((i + 1) * nq, nb - 1), c))
    return pl.pallas_call(
        kern,
        grid=(B, nb // nq),
        in_specs=[
            pl.BlockSpec((1, nq * blk, aw), cur(qb)),
            pl.BlockSpec((1, blk, kw), prev(kcb)),
            pl.BlockSpec((1, nq * blk, kw), cur(kcb)),
            pl.BlockSpec((1, blk, kw), nxt(kcb)),
            pl.BlockSpec((1, blk, kw), prev(vcb)),
            pl.BlockSpec((1, nq * blk, kw), cur(vcb)),
            pl.BlockSpec((1, blk, kw), nxt(vcb)),
            pl.BlockSpec((1, HEAD_DIM), lambda b, i: (0, 0)),
            pl.BlockSpec((1, HEAD_DIM), lambda b, i: (0, 0)),
            pl.BlockSpec((N_HEADS, blk, 3 * blk), lambda b, i: (0, 0, 0)),
            pl.BlockSpec((N_HEADS, 1, 1), lambda b, i: (0, 0, 0)),
        ],
        out_specs=pl.BlockSpec((1, nq * blk, aw), lambda b, i: (b, i, 0)),
        out_shape=jax.ShapeDtypeStruct((B, L, aw), BF16),
        compiler_params=_cparams(("parallel", "parallel")),
        name="attention",
    )(proj, proj, proj, proj, proj, proj, proj, q_g.reshape(1, -1), k_g.reshape(1, -1),
      _alibi_bias(), sink.astype(F32).reshape(N_HEADS, 1, 1))


def _merge_kernel(z_ref, a_ref, g_ref, x_ref, woh_ref, woa_ref, wout_ref, o_ref, *, tj):
    d = x_ref.shape[1]
    z = z_ref[...]
    a = a_ref[...]
    acc = x_ref[...]
    for j in range(d // tj):
        cols = slice(j * tj, (j + 1) * tj)
        y_hy = jnp.dot(z, woh_ref[:, cols], preferred_element_type=F32)
        y_at = jnp.dot(a, woa_ref[:, cols], preferred_element_type=F32)
        g_hy = jax.nn.sigmoid(g_ref[:, cols].astype(F32))
        g_at = jax.nn.sigmoid(g_ref[:, d + j * tj:d + (j + 1) * tj].astype(F32))
        mixed = (g_hy * y_hy + g_at * y_at).astype(BF16)
        acc = acc + jnp.dot(mixed, wout_ref[cols, :], preferred_element_type=F32)
    o_ref[...] = acc


def _merge(z, a, gates, x, w_oh, w_oa, w_out, *, tm, tj):
    T, D = x.shape
    cz, ca = z.shape[1], a.shape[1]
    kern = functools.partial(_merge_kernel, tj=tj)
    return pl.pallas_call(
        kern,
        grid=(T // tm,),
        in_specs=[
            pl.BlockSpec((tm, cz), lambda i: (i, 0)),
            pl.BlockSpec((tm, ca), lambda i: (i, 0)),
            pl.BlockSpec((tm, 2 * D), lambda i: (i, 0)),
            pl.BlockSpec((tm, D), lambda i: (i, 0)),
            pl.BlockSpec((cz, D), lambda i: (0, 0)),
            pl.BlockSpec((ca, D), lambda i: (0, 0)),
            pl.BlockSpec((D, D), lambda i: (0, 0)),
        ],
        out_specs=pl.BlockSpec((tm, D), lambda i: (i, 0)),
        out_shape=jax.ShapeDtypeStruct((T, D), F32),
        compiler_params=_cparams(("parallel",)),
        name="merge",
    )(z, a, gates, x, w_oh, w_oa, w_out)


def _mlp_kernel(x_ref, g_ref, wu_ref, wd_ref, o_ref, xn_ref):
    @pl.when(pl.program_id(1) == 0)
    def _():
        x = x_ref[...]
        ms = jnp.mean(x * x, axis=-1, keepdims=True)
        xn_ref[...] = (x * lax.rsqrt(ms + NORM_EPS) * g_ref[...]).astype(BF16)
        o_ref[...] = x

    h = jnp.dot(xn_ref[...], wu_ref[...], preferred_element_type=F32)
    h = jnp.square(jnp.maximum(h, 0.0)).astype(BF16)
    o_ref[...] += jnp.dot(h, wd_ref[...], preferred_element_type=F32)


def _mlp(x, g, w_up, w_down, *, tm, tf):
    T, D = x.shape
    d_ff = w_up.shape[1]
    return pl.pallas_call(
        _mlp_kernel,
        grid=(T // tm, d_ff // tf),
        in_specs=[
            pl.BlockSpec((tm, D), lambda i, j: (i, 0)),
            pl.BlockSpec((1, D), lambda i, j: (0, 0)),
            pl.BlockSpec((D, tf), lambda i, j: (0, j)),
            pl.BlockSpec((tf, D), lambda i, j: (j, 0)),
        ],
        out_specs=pl.BlockSpec((tm, D), lambda i, j: (i, 0)),
        out_shape=jax.ShapeDtypeStruct((T, D), F32),
        scratch_shapes=[pltpu.VMEM((tm, D), BF16)],
        compiler_params=_cparams(("parallel", "arbitrary")),
        name="mlp",
    )(x, g.reshape(1, D), w_up, w_down)


def _row_tile(T, pref):
    return pref if T % pref == 0 else T


def _encoder_layer(x, p, khat):
    B, L, D = x.shape
    T = B * L
    C = p["hyena_skip"].shape[1]
    off_q = 2 * D
    off_k = off_q + N_HEADS * HEAD_DIM
    off_v = off_k + N_KV_HEADS * HEAD_DIM
    x2 = x.reshape(T, D)
    tm = _row_tile(L, 1024)
    u = _in_proj_hyena(x2, p["norm_mix_g"], p["w_in_hyena"], p["conv_w"], p["conv_b"], B=B, L=L, C=C, tm=tm, tn=512)
    rest = _norm_matmul(x2, p["norm_mix_g"], p["w_in_rest"], tm=tm, tn=512, name="in_proj_rest", out_dtype=BF16)
    z = _hyena_mix(u, khat, p["hyena_skip"])
    att = _attention(rest.reshape(B, L, -1), p["q_norm_g"], p["k_norm_g"], p["attn_sink"],
                     off_q=off_q, off_k=off_k, off_v=off_v)
    x1 = _merge(z.reshape(T, C), att.reshape(T, -1), rest, x2, p["w_o_hyena"], p["w_o_attn"],
                p["w_out"], tm=_row_tile(T, 256), tj=512)
    y = _mlp(x1, p["norm_mlp_g"], p["w_up"], p["w_down"], tm=tm, tf=512)
    return y.reshape(B, L, D)


def kernel(x_prompt, x_sample, norm_mix_g, w_in, conv_w, conv_b, filt_w1, filt_b1, filt_w2, filt_b2,
           filt_w3, filt_freq, hyena_skip, q_norm_g, k_norm_g, attn_sink, w_o_hyena, w_o_attn, w_out,
           norm_mlp_g, w_up, w_down):
    depth = w_in.shape[0]
    D = x_prompt.shape[-1]
    C = hyena_skip.shape[2]
    off_g = w_in.shape[2] - 2 * D
    off_q = (HYENA_ORDER + 1) * C
    y_prompt, y_sample = x_prompt, x_sample
    for l in range(depth):
        p = dict(
            norm_mix_g=norm_mix_g[l], w_in_hyena=w_in[l, :, :off_q].astype(BF16),
            w_in_rest=jnp.concatenate([w_in[l, :, off_g:], w_in[l, :, off_q:off_g]], axis=1).astype(BF16),
            conv_w=conv_w[l], conv_b=conv_b[l],
            hyena_skip=hyena_skip[l], q_norm_g=q_norm_g[l], k_norm_g=k_norm_g[l], attn_sink=attn_sink[l],
            w_o_hyena=w_o_hyena[l].astype(BF16), w_o_attn=w_o_attn[l].astype(BF16),
            w_out=w_out[l].astype(BF16), norm_mlp_g=norm_mlp_g[l],
            w_up=w_up[l].astype(BF16), w_down=w_down[l].astype(BF16),
        )
        filt = (filt_w1[l], filt_b1[l], filt_w2[l], filt_b2[l], filt_w3[l], filt_freq[l])
        outs = []
        for x in (y_prompt, y_sample):
            khat = _filter_spectra(x.shape[1], C, *filt)
            outs.append(_encoder_layer(x, p, khat))
        y_prompt, y_sample = outs
    return (y_prompt, y_sample)
```

```python
import functools
import math

import numpy as np
import jax
import jax.numpy as jnp
from jax import lax
from jax.experimental import pallas as pl
from jax.experimental.pallas import tpu as pltpu

F32 = jnp.float32
BF16 = jnp.bfloat16

NORM_EPS = 1e-6
NEG_INF = -1e30

HYENA_ORDER = 2
N_HEADS = 8
N_KV_HEADS = 2
HEAD_REP = N_HEADS // N_KV_HEADS
HEAD_DIM = 128
WINDOW = 128
ATT_Q_BLOCKS = 4
FILT_BANDS = 16
DECAY_TARGET = 1e-2
MAX_DECAY = math.log(DECAY_TARGET) / 0.3
MIN_DECAY = math.log(DECAY_TARGET) / 1.5

FFT_N1 = 128
MXU_LANES = 256
KRON_F32 = 8
KRON = 16
MAX_K2_PER_STEP = 32
K2_UNROLL = 8
VMEM_LIMIT = 56 * 1024 * 1024


def _cparams(sem):
    return pltpu.CompilerParams(dimension_semantics=sem, vmem_limit_bytes=VMEM_LIMIT)


def _rms_rows(x, g):
    ms = jnp.mean(x * x, axis=-1, keepdims=True)
    return (x * lax.rsqrt(ms + NORM_EPS) * g).astype(BF16)


def _norm_matmul_kernel(x_ref, g_ref, w_ref, o_ref, *, tn):
    xn = _rms_rows(x_ref[...], g_ref[...])
    for j in range(w_ref.shape[1] // tn):
        cols = slice(j * tn, (j + 1) * tn)
        o_ref[:, cols] = jnp.dot(xn, w_ref[:, cols], preferred_element_type=F32).astype(o_ref.dtype)


def _norm_matmul(x, g, w, *, tm, tn, name, out_dtype):
    T, D = x.shape
    n_out = w.shape[1]
    return pl.pallas_call(
        functools.partial(_norm_matmul_kernel, tn=tn),
        grid=(T // tm,),
        in_specs=[
            pl.BlockSpec((tm, D), lambda i: (i, 0)),
            pl.BlockSpec((1, D), lambda i: (0, 0)),
            pl.BlockSpec((D, n_out), lambda i: (0, 0)),
        ],
        out_specs=pl.BlockSpec((tm, n_out), lambda i: (i, 0)),
        out_shape=jax.ShapeDtypeStruct((T, n_out), out_dtype),
        compiler_params=_cparams(("parallel",)),
        name=name,
    )(x, g.reshape(1, D), w)


HALO = 16


def _in_proj_hyena_kernel(x_ref, xp_ref, xn_ref, g_ref, w_ref, cw_ref, cb_ref, o_ref, xs_ref, *, blocks_per_seq, tn):
    tm = x_ref.shape[0]
    c = o_ref.shape[-1]
    r = pl.program_id(0) % blocks_per_seq
    g = g_ref[...]
    xs_ref[0:HALO, :] = jnp.where(r > 0, _rms_rows(xp_ref[...], g), jnp.zeros((), BF16))
    xs_ref[HALO:HALO + tm, :] = _rms_rows(x_ref[...], g)
    xs_ref[HALO + tm:, :] = jnp.where(r < blocks_per_seq - 1, _rms_rows(xn_ref[...], g), jnp.zeros((), BF16))
    rows = tm + 2 * HALO
    for j in range(w_ref.shape[1] // tn):
        cols = slice(j * tn, (j + 1) * tn)
        p = jnp.dot(xs_ref[...], w_ref[:, cols], preferred_element_type=F32)
        prev = pltpu.roll(p, 1, axis=0)[HALO:HALO + tm]
        nxt = pltpu.roll(p, rows - 1, axis=0)[HALO:HALO + tm]
        u = prev * cw_ref[0:1, cols] + p[HALO:HALO + tm] * cw_ref[1:2, cols] + nxt * cw_ref[2:3, cols] + cb_ref[:, cols]
        comp, c0 = divmod(j * tn, c)
        o_ref[comp, 0, :, c0:c0 + tn] = u.astype(o_ref.dtype)


def _in_proj_hyena(x, g, w, conv_w, conv_b, *, B, L, C, tm, tn):
    T, D = x.shape
    bps = L // tm
    hpb = tm // HALO
    n_out = w.shape[1]
    kern = functools.partial(_in_proj_hyena_kernel, blocks_per_seq=bps, tn=tn)
    return pl.pallas_call(
        kern,
        grid=(T // tm,),
        in_specs=[
            pl.BlockSpec((tm, D), lambda i: (i, 0)),
            pl.BlockSpec((HALO, D), lambda i: (jnp.maximum(i * hpb - 1, 0), 0)),
            pl.BlockSpec((HALO, D), lambda i: (jnp.minimum((i + 1) * hpb, T // HALO - 1), 0)),
            pl.BlockSpec((1, D), lambda i: (0, 0)),
            pl.BlockSpec((D, n_out), lambda i: (0, 0)),
            pl.BlockSpec((3, n_out), lambda i: (0, 0)),
            pl.BlockSpec((1, n_out), lambda i: (0, 0)),
        ],
        out_specs=pl.BlockSpec((n_out // C, 1, tm, C), lambda i: (0, i // bps, i % bps, 0)),
        out_shape=jax.ShapeDtypeStruct((n_out // C, B, L, C), BF16),
        scratch_shapes=[pltpu.VMEM((tm + 2 * HALO, D), BF16)],
        compiler_params=_cparams(("parallel",)),
        name="in_proj_hyena",
    )(x, x, x, g.reshape(1, D), w, conv_w, conv_b.reshape(1, -1))


def _filter_kernel(pos_ref, t_ref, w1_ref, b1_ref, w2_ref, b2_ref, w3_ref, fr_ref,
                   dl_ref, bw_ref, o_ref, h_ref):
    hi = lax.Precision.HIGHEST

    @pl.when(pl.program_id(1) == 0)
    def _():
        fr = fr_ref[...]
        a = jnp.dot(pos_ref[...], w1_ref[...], precision=hi, preferred_element_type=F32) + b1_ref[...]
        a = jnp.sin(fr * a)
        a = jnp.dot(a, w2_ref[...], precision=hi, preferred_element_type=F32) + b2_ref[...]
        h_ref[...] = jnp.sin(fr * a).astype(BF16)

    h = jnp.dot(h_ref[...], w3_ref[...], preferred_element_type=F32)
    h = h * jnp.exp(-t_ref[...] * dl_ref[...])
    row = lax.broadcasted_iota(jnp.int32, h.shape, 0)
    first = (row == 0) & (pl.program_id(0) == 0)
    o_ref[...] = jnp.where(first & (bw_ref[...] > 0.0), 0.0, h).astype(o_ref.dtype)


def _filter_gen(L, C, w1, b1, w2, b2, w3, freq, *, tl, tc):
    emb = w1.shape[0]
    hid = w1.shape[1]
    n_f = w3.shape[1]
    t = np.linspace(0.0, 1.0, L, dtype=np.float32)[:, None]
    w = (2.0 * math.pi * np.arange(L, dtype=np.float32)[:, None] / L).astype(np.float32)
    f = np.linspace(1e-4, FILT_BANDS - 1, FILT_BANDS, dtype=np.float32)[None, :]
    fw = (f * w).astype(np.float32)
    pos = np.zeros((L, 128), np.float32)
    pos[:, :emb] = np.concatenate([t, np.cos(fw), -np.sin(fw)], axis=-1)
    deltas = np.abs(np.linspace(MIN_DECAY, MAX_DECAY, C, dtype=np.float32))
    dl = np.tile(deltas, n_f // C)[None, :]
    bw = np.tile(np.concatenate([np.zeros(C, np.float32), np.ones(C, np.float32)]), n_f // (2 * C))[None, :]
    w1p = jnp.zeros((128, hid), F32).at[:emb].set(w1)
    return pl.pallas_call(
        _filter_kernel,
        grid=(L // tl, n_f // tc),
        in_specs=[
            pl.BlockSpec((tl, 128), lambda i, j: (i, 0)),
            pl.BlockSpec((tl, 1), lambda i, j: (i, 0)),
            pl.BlockSpec((128, hid), lambda i, j: (0, 0)),
            pl.BlockSpec((1, hid), lambda i, j: (0, 0)),
            pl.BlockSpec((hid, hid), lambda i, j: (0, 0)),
            pl.BlockSpec((1, hid), lambda i, j: (0, 0)),
            pl.BlockSpec((hid, tc), lambda i, j: (0, j)),
            pl.BlockSpec((1, hid), lambda i, j: (0, 0)),
            pl.BlockSpec((1, tc), lambda i, j: (0, j)),
            pl.BlockSpec((1, tc), lambda i, j: (0, j)),
        ],
        out_specs=pl.BlockSpec((tl, tc), lambda i, j: (i, j)),
        out_shape=jax.ShapeDtypeStruct((L, n_f), BF16),
        scratch_shapes=[pltpu.VMEM((tl, hid), BF16)],
        compiler_params=_cparams(("parallel", "arbitrary")),
        name="filter_gen",
    )(jnp.asarray(pos), jnp.asarray(t), w1p, b1.reshape(1, hid), w2, b2.reshape(1, hid), w3.astype(BF16),
      freq.reshape(1, hid), jnp.asarray(dl), jnp.asarray(bw))


@functools.lru_cache(maxsize=None)
def _dft_tables(L):
    N = 2 * L
    N1 = FFT_N1
    N2 = N // N1
    H2 = N2 // 2
    k2 = np.arange(H2, dtype=np.float64)
    n2 = np.arange(H2, dtype=np.float64)
    th = 2.0 * np.pi * np.outer(k2 + 0.5, n2) / N2
    ma = np.stack([np.cos(th), -np.sin(th)], axis=1).reshape(N2, H2)
    ga = ma.T * (2.0 / N)
    mak = np.kron(ma, np.eye(KRON_F32))
    gak = np.kron(ga, np.eye(KRON))
    k1 = np.arange(N1, dtype=np.float64)
    n1 = np.arange(N1, dtype=np.float64)
    kk = k1[None, :, None] * N2 + k2[:, None, None] + 0.5
    ph = 2.0 * np.pi * kk * n1[None, None, :] / N
    c, s = np.cos(ph), np.sin(ph)
    fwd = np.concatenate([np.concatenate([c, s], axis=2), np.concatenate([-s, c], axis=2)], axis=1)
    ct, st = np.swapaxes(c, 1, 2), np.swapaxes(s, 1, 2)
    inv = np.concatenate([np.concatenate([ct, -st], axis=2), np.concatenate([st, ct], axis=2)], axis=1)
    return tuple(a.astype(np.float32) for a in (mak, gak, fwd, inv))


def _conv_plan(L):
    H2 = L // FFT_N1
    long_seq = H2 > MAX_K2_PER_STEP
    kb = min(H2, MAX_K2_PER_STEP // 2 if long_seq else MAX_K2_PER_STEP)
    return H2, kb, MXU_LANES, 2 if long_seq else 1


def _mxu_tables(L):
    H2, kb, _, _ = _conv_plan(L)
    mak, gak, fwd, inv = (jnp.asarray(a).astype(BF16) for a in _dft_tables(L))
    return mak.reshape(H2 // kb, 2 * kb * KRON_F32, H2 * KRON_F32), gak, fwd, inv


def _stage_a(z_ref, ma_ref, a_ref):
    h2, tc = z_ref.shape[1], z_ref.shape[3]
    kf = KRON_F32
    for gp in range(FFT_N1 // KRON):
        rows16 = z_ref[0, :, gp * KRON:(gp + 1) * KRON, :].astype(F32)
        parts = []
        for g in range(KRON // kf):
            slab = rows16[:, g * kf:(g + 1) * kf, :].reshape(h2 * kf, tc).astype(BF16)
            r = jnp.dot(ma_ref[0], slab, preferred_element_type=F32)
            parts.append(r.reshape(-1, kf, tc))
        a_ref[:, gp * KRON:(gp + 1) * KRON, :] = jnp.concatenate(parts, axis=1).astype(BF16)


def _conv_fwd_kernel(z_ref, ma_ref, f_ref, g_ref, k_ref, o_ref, a_ref, *, kb):
    n1 = FFT_N1
    tc = z_ref.shape[-1]
    _stage_a(z_ref.at[0], ma_ref, a_ref)

    def body(i, carry):
        d = a_ref[pl.ds(2 * i, 2)].reshape(2 * n1, tc)
        x = jnp.dot(f_ref[i], d, preferred_element_type=F32)
        xr, xi = x[:n1], x[n1:]
        kr, ki = k_ref[0, i, 0], k_ref[0, i, 1]
        y = jnp.concatenate([xr * kr - xi * ki, xr * ki + xi * kr], axis=0).astype(BF16)
        c = jnp.dot(g_ref[i], y, preferred_element_type=F32).astype(BF16)
        o_ref[0, i] = c.reshape(2, n1, tc)
        return carry

    lax.fori_loop(0, kb, body, 0, unroll=K2_UNROLL)


def _conv_fwd(z5, comp, mak, fwd, inv, khat, order):
    _, B, H2, N1, C = z5.shape
    _, kb, tc, _ = _conv_plan(H2 * N1)
    kern = functools.partial(_conv_fwd_kernel, kb=kb)
    return pl.pallas_call(
        kern,
        grid=(C // tc, B, H2 // kb),
        in_specs=[
            pl.BlockSpec((1, 1, H2, N1, tc), lambda c, b, k: (comp, b, 0, 0, c)),
            pl.BlockSpec((1, 2 * kb * KRON_F32, H2 * KRON_F32), lambda c, b, k: (k, 0, 0)),
            pl.BlockSpec((kb, 2 * N1, 2 * N1), lambda c, b, k: (k, 0, 0)),
            pl.BlockSpec((kb, 2 * N1, 2 * N1), lambda c, b, k: (k, 0, 0)),
            pl.BlockSpec((1, kb, 2, N1, tc), lambda c, b, k: (order, k, 0, 0, c)),
        ],
        out_specs=pl.BlockSpec((1, kb, 2, N1, tc), lambda c, b, k: (b, k, 0, 0, c)),
        out_shape=jax.ShapeDtypeStruct((B, H2, 2, N1, C), BF16),
        scratch_shapes=[pltpu.VMEM((2 * kb, N1, tc), BF16)],
        compiler_params=_cparams(("parallel", "parallel", "arbitrary")),
        name="conv_fwd",
    )(z5, mak, fwd, inv, khat)


def _conv_inv_kernel(t_ref, ga_ref, z_ref, gate_ref, skip_ref, o_ref):
    h2, rows, tc = z_ref.shape[2], z_ref.shape[3], z_ref.shape[4]
    for g in range(rows // KRON):
        sl = slice(g * KRON, (g + 1) * KRON)
        slab = t_ref[0, :, :, sl, :].reshape(2 * h2 * KRON, tc)
        y = jnp.dot(ga_ref[...], slab, preferred_element_type=F32).reshape(h2, KRON, tc)
        out = gate_ref[0, 0, :, sl, :].astype(F32) * (y + skip_ref[...] * z_ref[0, 0, :, sl, :].astype(F32))
        o_ref[0, 0, :, sl, :] = out.astype(o_ref.dtype)


def _conv_inv(t2, gak, z5, zcomp, gate5, gcomp, skip_row, out_dtype):
    B, H2, _, N1, C = t2.shape
    _, _, tc, ns = _conv_plan(H2 * N1)
    return pl.pallas_call(
        _conv_inv_kernel,
        grid=(C // tc, B, ns),
        in_specs=[
            pl.BlockSpec((1, H2, 2, N1 // ns, tc), lambda c, b, s: (b, 0, 0, s, c)),
            pl.BlockSpec((H2 * KRON, 2 * H2 * KRON), lambda c, b, s: (0, 0)),
            pl.BlockSpec((1, 1, H2, N1 // ns, tc), lambda c, b, s: (zcomp, b, 0, s, c)),
            pl.BlockSpec((1, 1, H2, N1 // ns, tc), lambda c, b, s: (gcomp, b, 0, s, c)),
            pl.BlockSpec((1, tc), lambda c, b, s: (0, c)),
        ],
        out_specs=pl.BlockSpec((1, 1, H2, N1 // ns, tc), lambda c, b, s: (0, b, 0, s, c)),
        out_shape=jax.ShapeDtypeStruct((1, B, H2, N1, C), out_dtype),
        compiler_params=_cparams(("parallel", "parallel", "parallel")),
        name="conv_inv",
    )(t2, gak, z5, gate5, skip_row)


def _filter_fwd_kernel(hf_ref, hb_ref, ma_ref, f_ref, o_ref, af_ref, ab_ref, *, kb):
    n1 = FFT_N1
    tc = hf_ref.shape[-1]
    _stage_a(hf_ref, ma_ref, af_ref)
    _stage_a(hb_ref, ma_ref, ab_ref)

    def body(i, carry):
        xf = jnp.dot(f_ref[i], af_ref[pl.ds(2 * i, 2)].reshape(2 * n1, tc), preferred_element_type=F32)
        xb = jnp.dot(f_ref[i], ab_ref[pl.ds(2 * i, 2)].reshape(2 * n1, tc), preferred_element_type=F32)
        o_ref[0, i, 0] = xf[:n1] + xb[:n1]
        o_ref[0, i, 1] = xf[n1:] - xb[n1:]
        return carry

    lax.fori_loop(0, kb, body, 0, unroll=K2_UNROLL)


def _filter_fwd(h4, mak, fwd, C):
    _, H2, N1, n_f = h4.shape
    _, kb, tc, _ = _conv_plan(H2 * N1)
    n_order = n_f // (2 * C)
    cpb = C // tc
    kern = functools.partial(_filter_fwd_kernel, kb=kb)
    return pl.pallas_call(
        kern,
        grid=(n_order, cpb, H2 // kb),
        in_specs=[
            pl.BlockSpec((1, H2, N1, tc), lambda o, c, k: (0, 0, 0, (2 * o) * cpb + c)),
            pl.BlockSpec((1, H2, N1, tc), lambda o, c, k: (0, 0, 0, (2 * o + 1) * cpb + c)),
            pl.BlockSpec((1, 2 * kb * KRON_F32, H2 * KRON_F32), lambda o, c, k: (k, 0, 0)),
            pl.BlockSpec((kb, 2 * N1, 2 * N1), lambda o, c, k: (k, 0, 0)),
        ],
        out_specs=pl.BlockSpec((1, kb, 2, N1, tc), lambda o, c, k: (o, k, 0, 0, c)),
        out_shape=jax.ShapeDtypeStruct((n_order, H2, 2, N1, C), F32),
        scratch_shapes=[pltpu.VMEM((2 * kb, N1, tc), BF16), pltpu.VMEM((2 * kb, N1, tc), BF16)],
        compiler_params=_cparams(("parallel", "parallel", "arbitrary")),
        name="filter_fwd",
    )(h4, h4, mak, fwd)


def _filter_spectra(L, C, w1, b1, w2, b2, w3, freq):
    N1 = FFT_N1
    H2 = L // N1
    mak, _, fwd, _ = _mxu_tables(L)
    n_f = w3.shape[1]
    h = _filter_gen(L, C, w1, b1, w2, b2, w3, freq, tl=min(512, L), tc=min(1024, n_f))
    return _filter_fwd(h.reshape(1, H2, N1, n_f), mak, fwd, C)


def _hyena_mix(u, khat, skip):
    _, B, L, C = u.shape
    N1 = FFT_N1
    H2 = L // N1
    mak, gak, fwd, inv = _mxu_tables(L)
    u5 = u.reshape(3, B, H2, N1, C)
    z5, zc = u5, 0
    for o in range(HYENA_ORDER):
        t2 = _conv_fwd(z5, zc, mak, fwd, inv, khat, o)
        z5 = _conv_inv(t2, gak, z5, zc, u5, o + 1, skip[o].astype(F32).reshape(1, C), BF16)
        zc = 0
    return z5.reshape(B, L, C)


@functools.lru_cache(maxsize=None)
def _alibi_bias():
    q = np.arange(WINDOW)[:, None]
    kpos = np.arange(3 * WINDOW)[None, :] - WINDOW
    dist = np.abs(q - kpos).astype(np.float32)
    slopes = np.exp2(-8.0 * np.arange(1, N_HEADS + 1, dtype=np.float32) / N_HEADS)
    bias = -slopes[:, None, None] * dist[None]
    bias = np.where(dist[None] <= WINDOW, bias, NEG_INF).astype(np.float32)
    return bias


def _row_sumsq(x):
    sq = x * x
    hi = sq.astype(BF16)
    lo = (sq - hi.astype(F32)).astype(BF16)
    ones = jnp.ones((x.shape[-1], x.shape[-1]), BF16)
    return (jnp.dot(hi, ones, preferred_element_type=F32) + jnp.dot(lo, ones, preferred_element_type=F32))


def _head_norm(x, g):
    return x * lax.rsqrt(_row_sumsq(x) * (1.0 / x.shape[-1]) + NORM_EPS) * g


def _attention_kernel(q_ref, kp_ref, kc_ref, kn_ref, vp_ref, vc_ref, vn_ref, qg_ref, kg_ref,
                      bias_ref, sink_ref, o_ref, *, n_blocks, nq):
    i = pl.program_id(1)
    hd, blk, rep = HEAD_DIM, WINDOW, HEAD_REP
    kk = jnp.concatenate([kp_ref[0], kc_ref[0], kn_ref[0]], axis=0)
    vv = jnp.concatenate([vp_ref[0], vc_ref[0], vn_ref[0]], axis=0)
    col = lax.broadcasted_iota(jnp.int32, (1, 3 * blk), 1)
    for g in range(N_KV_HEADS):
        kn_all = _head_norm(kk[:, g * hd:(g + 1) * hd].astype(F32), kg_ref[...]).astype(BF16)
        vg_all = jnp.concatenate([vv[:, g * hd:(g + 1) * hd].astype(BF16), jnp.ones((kk.shape[0], hd), BF16)], axis=1)
        sink = sink_ref[g * rep:(g + 1) * rep]
        for t in range(nq):
            gi = i * nq + t
            in_seq = ((col >= blk) | (gi > 0)) & ((col < 2 * blk) | (gi < n_blocks - 1))
            rows = slice(t * blk, (t + 1) * blk)
            kn = kn_all[t * blk:(t + 3) * blk]
            vg = vg_all[t * blk:(t + 3) * blk]
            qs = jnp.concatenate(
                [q_ref[0, rows, (g * rep + r) * hd:(g * rep + r + 1) * hd] for r in range(rep)], axis=0)
            qn = _head_norm(qs.astype(F32), qg_ref[...]).astype(BF16)
            s = lax.dot_general(qn, kn, (((1,), (1,)), ((), ())), preferred_element_type=F32)
            s = s.reshape(rep, blk, 3 * blk) * (hd ** -0.5) + bias_ref[g * rep:(g + 1) * rep]
            s = jnp.where(in_seq, s, NEG_INF)
            m = jnp.maximum(jnp.max(s, axis=-1, keepdims=True), sink)
            p = jnp.exp(s - m)
            pv = jnp.dot(p.reshape(rep * blk, 3 * blk).astype(BF16), vg, preferred_element_type=F32)
            pv = pv.reshape(rep, blk, 2 * hd)
            denom = pv[:, :, hd:] + jnp.exp(sink - m)
            o = pv[:, :, :hd] / denom
            for r in range(rep):
                o_ref[0, rows, (g * rep + r) * hd:(g * rep + r + 1) * hd] = o[r].astype(o_ref.dtype)


def _attention(proj, q_g, k_g, sink, *, off_q, off_k, off_v):
    B, L, _ = proj.shape
    blk = WINDOW
    nb = L // blk
    nq = min(ATT_Q_BLOCKS, nb)
    aw = N_HEADS * HEAD_DIM
    kw = N_KV_HEADS * HEAD_DIM
    qb, kcb, vcb = off_q // aw, off_k // kw, off_v // kw
    kern = functools.partial(_attention_kernel, n_blocks=nb, nq=nq)
    prev = lambda c: (lambda b, i: (b, jnp.maximum(i * nq - 1, 0), c))
    cur = lambda c: (lambda b, i: (b, i, c))
    nxt = lambda c: (lambda b, i: (b, jnp.minimum((i + 1) * nq, nb - 1), c))
    return pl.pallas_call(
        kern,
        grid=(B, nb // nq),
        in_specs=[
            pl.BlockSpec((1, nq * blk, aw), cur(qb)),
            pl.BlockSpec((1, blk, kw), prev(kcb)),
            pl.BlockSpec((1, nq * blk, kw), cur(kcb)),
            pl.BlockSpec((1, blk, kw), nxt(kcb)),
            pl.BlockSpec((1, blk, kw), prev(vcb)),
            pl.BlockSpec((1, nq * blk, kw), cur(vcb)),
            pl.BlockSpec((1, blk, kw), nxt(vcb)),
            pl.BlockSpec((1, HEAD_DIM), lambda b, i: (0, 0)),
            pl.BlockSpec((1, HEAD_DIM), lambda b, i: (0, 0)),
            pl.BlockSpec((N_HEADS, blk, 3 * blk), lambda b, i: (0, 0, 0)),
            pl.BlockSpec((N_HEADS, 1, 1), lambda b, i: (0, 0, 0)),
        ],
        out_specs=pl.BlockSpec((1, nq * blk, aw), lambda b, i: (b, i, 0)),
        out_shape=jax.ShapeDtypeStruct((B, L, aw), BF16),
        compiler_params=_cparams(("parallel", "parallel")),
        name="attention",
    )(proj, proj, proj, proj, proj, proj, proj, q_g.reshape(1, -1), k_g.reshape(1, -1),
      _alibi_bias(), sink.astype(F32).reshape(N_HEADS, 1, 1))


def _merge_kernel(z_ref, a_ref, g_ref, x_ref, woh_ref, woa_ref, wout_ref, o_ref, *, tj):
    d = x_ref.shape[1]
    z = z_ref[...]
    a = a_ref[...]
    acc = x_ref[...]
    for j in range(d // tj):
        cols = slice(j * tj, (j + 1) * tj)
        y_hy = jnp.dot(z, woh_ref[:, cols], preferred_element_type=F32)
        y_at = jnp.dot(a, woa_ref[:, cols], preferred_element_type=F32)
        g_hy = jax.nn.sigmoid(g_ref[:, cols].astype(F32))
        g_at = jax.nn.sigmoid(g_ref[:, d + j * tj:d + (j + 1) * tj].astype(F32))
        mixed = (g_hy * y_hy + g_at * y_at).astype(BF16)
        acc = acc + jnp.dot(mixed, wout_ref[cols, :], preferred_element_type=F32)
    o_ref[...] = acc


def _merge(z, a, gates, x, w_oh, w_oa, w_out, *, tm, tj):
    T, D = x.shape
    cz, ca = z.shape[1], a.shape[1]
    kern = functools.partial(_merge_kernel, tj=tj)
    return pl.pallas_call(
        kern,
        grid=(T // tm,),
        in_specs=[
            pl.BlockSpec((tm, cz), lambda i: (i, 0)),
            pl.BlockSpec((tm, ca), lambda i: (i, 0)),
            pl.BlockSpec((tm, 2 * D), lambda i: (i, 0)),
            pl.BlockSpec((tm, D), lambda i: (i, 0)),
            pl.BlockSpec((cz, D), lambda i: (0, 0)),
            pl.BlockSpec((ca, D), lambda i: (0, 0)),
            pl.BlockSpec((D, D), lambda i: (0, 0)),
        ],
        out_specs=pl.BlockSpec((tm, D), lambda i: (i, 0)),
        out_shape=jax.ShapeDtypeStruct((T, D), F32),
        compiler_params=_cparams(("parallel",)),
        name="merge",
    )(z, a, gates, x, w_oh, w_oa, w_out)


def _mlp_kernel(x_ref, g_ref, wu_ref, wd_ref, o_ref, xn_ref):
    @pl.when(pl.program_id(1) == 0)
    def _():
        x = x_ref[...]
        ms = jnp.mean(x * x, axis=-1, keepdims=True)
        xn_ref[...] = (x * lax.rsqrt(ms + NORM_EPS) * g_ref[...]).astype(BF16)
        o_ref[...] = x

    h = jnp.dot(xn_ref[...], wu_ref[...], preferred_element_type=F32)
    h = jnp.square(jnp.maximum(h, 0.0)).astype(BF16)
    o_ref[...] += jnp.dot(h, wd_ref[...], preferred_element_type=F32)


def _mlp(x, g, w_up, w_down, *, tm, tf):
    T, D = x.shape
    d_ff = w_up.shape[1]
    return pl.pallas_call(
        _mlp_kernel,
        grid=(T // tm, d_ff // tf),
        in_specs=[
            pl.BlockSpec((tm, D), lambda i, j: (i, 0)),
            pl.BlockSpec((1, D), lambda i, j: (0, 0)),
            pl.BlockSpec((D, tf), lambda i, j: (0, j)),
            pl.BlockSpec((tf, D), lambda i, j: (j, 0)),
        ],
        out_specs=pl.BlockSpec((tm, D), lambda i, j: (i, 0)),
        out_shape=jax.ShapeDtypeStruct((T, D), F32),
        scratch_shapes=[pltpu.VMEM((tm, D), BF16)],
        compiler_params=_cparams(("parallel", "arbitrary")),
        name="mlp",
    )(x, g.reshape(1, D), w_up, w_down)


def _row_tile(T, pref):
    return pref if T % pref == 0 else T


def _encoder_layer(x, p, khat):
    B, L, D = x.shape
    T = B * L
    C = p["hyena_skip"].shape[1]
    off_q = 2 * D
    off_k = off_q + N_HEADS * HEAD_DIM
    off_v = off_k + N_KV_HEADS * HEAD_DIM
    x2 = x.reshape(T, D)
    tm = _row_tile(L, 1024)
    u = _in_proj_hyena(x2, p["norm_mix_g"], p["w_in_hyena"], p["conv_w"], p["conv_b"], B=B, L=L, C=C, tm=tm, tn=512)
    rest = _norm_matmul(x2, p["norm_mix_g"], p["w_in_rest"], tm=_row_tile(T, 512), tn=512, name="in_proj_rest",
                        out_dtype=BF16)
    z = _hyena_mix(u, khat, p["hyena_skip"])
    att = _attention(rest.reshape(B, L, -1), p["q_norm_g"], p["k_norm_g"], p["attn_sink"],
                     off_q=off_q, off_k=off_k, off_v=off_v)
    x1 = _merge(z.reshape(T, C), att.reshape(T, -1), rest, x2, p["w_o_hyena"], p["w_o_attn"],
                p["w_out"], tm=_row_tile(T, 512), tj=512)
    y = _mlp(x1, p["norm_mlp_g"], p["w_up"], p["w_down"], tm=tm, tf=512)
    return y.reshape(B, L, D)


def kernel(x_prompt, x_sample, norm_mix_g, w_in, conv_w, conv_b, filt_w1, filt_b1, filt_w2, filt_b2,
           filt_w3, filt_freq, hyena_skip, q_norm_g, k_norm_g, attn_sink, w_o_hyena, w_o_attn, w_out,
           norm_mlp_g, w_up, w_down):
    depth = w_in.shape[0]
    D = x_prompt.shape[-1]
    C = hyena_skip.shape[2]
    off_g = w_in.shape[2] - 2 * D
    off_q = (HYENA_ORDER + 1) * C
    y_prompt, y_sample = x_prompt, x_sample
    for l in range(depth):
        p = dict(
            norm_mix_g=norm_mix_g[l], w_in_hyena=w_in[l, :, :off_q].astype(BF16),
            w_in_rest=jnp.concatenate([w_in[l, :, off_g:], w_in[l, :, off_q:off_g]], axis=1).astype(BF16),
            conv_w=conv_w[l], conv_b=conv_b[l],
            hyena_skip=hyena_skip[l], q_norm_g=q_norm_g[l], k_norm_g=k_norm_g[l], attn_sink=attn_sink[l],
            w_o_hyena=w_o_hyena[l].astype(BF16), w_o_attn=w_o_attn[l].astype(BF16),
            w_out=w_out[l].astype(BF16), norm_mlp_g=norm_mlp_g[l],
            w_up=w_up[l].astype(BF16), w_down=w_down[l].astype(BF16),
        )
        filt = (filt_w1[l], filt_b1[l], filt_w2[l], filt_b2[l], filt_w3[l], filt_freq[l])
        outs = []
        for x in (y_prompt, y_sample):
            khat = _filter_spectra(x.shape[1], C, *filt)
            outs.append(_encoder_layer(x, p, khat))
        y_prompt, y_sample = outs
    return (y_prompt, y_sample)
```

```python
import functools
import math

import numpy as np
import jax
import jax.numpy as jnp
from jax import lax
from jax.experimental import pallas as pl
from jax.experimental.pallas import tpu as pltpu

F32 = jnp.float32
BF16 = jnp.bfloat16

NORM_EPS = 1e-6
NEG_INF = -1e30

HYENA_ORDER = 2
N_HEADS = 8
N_KV_HEADS = 2
HEAD_REP = N_HEADS // N_KV_HEADS
HEAD_DIM = 128
WINDOW = 128
ATT_Q_BLOCKS = 4
FILT_BANDS = 16
DECAY_TARGET = 1e-2
MAX_DECAY = math.log(DECAY_TARGET) / 0.3
MIN_DECAY = math.log(DECAY_TARGET) / 1.5

FFT_N1 = 128
MXU_LANES = 256
KRON_F32 = 8
KRON = 16
MAX_K2_PER_STEP = 32
K2_UNROLL = 16
VMEM_LIMIT = 56 * 1024 * 1024


def _cparams(sem):
    return pltpu.CompilerParams(dimension_semantics=sem, vmem_limit_bytes=VMEM_LIMIT)


def _rms_rows(x, g):
    ms = jnp.mean(x * x, axis=-1, keepdims=True)
    return (x * lax.rsqrt(ms + NORM_EPS) * g).astype(BF16)


def _norm_matmul_kernel(x_ref, g_ref, w_ref, o_ref, *, tn):
    xn = _rms_rows(x_ref[...], g_ref[...])
    for j in range(w_ref.shape[1] // tn):
        cols = slice(j * tn, (j + 1) * tn)
        o_ref[:, cols] = jnp.dot(xn, w_ref[:, cols], preferred_element_type=F32).astype(o_ref.dtype)


def _norm_matmul(x, g, w, *, tm, tn, name, out_dtype):
    T, D = x.shape
    n_out = w.shape[1]
    return pl.pallas_call(
        functools.partial(_norm_matmul_kernel, tn=tn),
        grid=(T // tm,),
        in_specs=[
            pl.BlockSpec((tm, D), lambda i: (i, 0)),
            pl.BlockSpec((1, D), lambda i: (0, 0)),
            pl.BlockSpec((D, n_out), lambda i: (0, 0)),
        ],
        out_specs=pl.BlockSpec((tm, n_out), lambda i: (i, 0)),
        out_shape=jax.ShapeDtypeStruct((T, n_out), out_dtype),
        compiler_params=_cparams(("parallel",)),
        name=name,
    )(x, g.reshape(1, D), w)


HALO = 16


def _in_proj_hyena_kernel(x_ref, xp_ref, xn_ref, g_ref, w_ref, cw_ref, cb_ref, o_ref, xs_ref, *, blocks_per_seq, tn):
    tm = x_ref.shape[0]
    c = o_ref.shape[-1]
    r = pl.program_id(0) % blocks_per_seq
    g = g_ref[...]
    xs_ref[0:HALO, :] = jnp.where(r > 0, _rms_rows(xp_ref[...], g), jnp.zeros((), BF16))
    xs_ref[HALO:HALO + tm, :] = _rms_rows(x_ref[...], g)
    xs_ref[HALO + tm:, :] = jnp.where(r < blocks_per_seq - 1, _rms_rows(xn_ref[...], g), jnp.zeros((), BF16))
    rows = tm + 2 * HALO
    for j in range(w_ref.shape[1] // tn):
        cols = slice(j * tn, (j + 1) * tn)
        p = jnp.dot(xs_ref[...], w_ref[:, cols], preferred_element_type=F32)
        prev = pltpu.roll(p, 1, axis=0)[HALO:HALO + tm]
        nxt = pltpu.roll(p, rows - 1, axis=0)[HALO:HALO + tm]
        u = prev * cw_ref[0:1, cols] + p[HALO:HALO + tm] * cw_ref[1:2, cols] + nxt * cw_ref[2:3, cols] + cb_ref[:, cols]
        comp, c0 = divmod(j * tn, c)
        o_ref[comp, 0, :, c0:c0 + tn] = u.astype(o_ref.dtype)


def _in_proj_hyena(x, g, w, conv_w, conv_b, *, B, L, C, tm, tn):
    T, D = x.shape
    bps = L // tm
    hpb = tm // HALO
    n_out = w.shape[1]
    kern = functools.partial(_in_proj_hyena_kernel, blocks_per_seq=bps, tn=tn)
    return pl.pallas_call(
        kern,
        grid=(T // tm,),
        in_specs=[
            pl.BlockSpec((tm, D), lambda i: (i, 0)),
            pl.BlockSpec((HALO, D), lambda i: (jnp.maximum(i * hpb - 1, 0), 0)),
            pl.BlockSpec((HALO, D), lambda i: (jnp.minimum((i + 1) * hpb, T // HALO - 1), 0)),
            pl.BlockSpec((1, D), lambda i: (0, 0)),
            pl.BlockSpec((D, n_out), lambda i: (0, 0)),
            pl.BlockSpec((3, n_out), lambda i: (0, 0)),
            pl.BlockSpec((1, n_out), lambda i: (0, 0)),
        ],
        out_specs=pl.BlockSpec((n_out // C, 1, tm, C), lambda i: (0, i // bps, i % bps, 0)),
        out_shape=jax.ShapeDtypeStruct((n_out // C, B, L, C), BF16),
        scratch_shapes=[pltpu.VMEM((tm + 2 * HALO, D), BF16)],
        compiler_params=_cparams(("parallel",)),
        name="in_proj_hyena",
    )(x, x, x, g.reshape(1, D), w, conv_w, conv_b.reshape(1, -1))


def _filter_kernel(pos_ref, t_ref, w1_ref, b1_ref, w2_ref, b2_ref, w3_ref, fr_ref,
                   dl_ref, bw_ref, o_ref, h_ref):
    hi = lax.Precision.HIGHEST

    @pl.when(pl.program_id(1) == 0)
    def _():
        fr = fr_ref[...]
        a = jnp.dot(pos_ref[...], w1_ref[...], precision=hi, preferred_element_type=F32) + b1_ref[...]
        a = jnp.sin(fr * a)
        a = jnp.dot(a, w2_ref[...], precision=hi, preferred_element_type=F32) + b2_ref[...]
        h_ref[...] = jnp.sin(fr * a).astype(BF16)

    h = jnp.dot(h_ref[...], w3_ref[...], preferred_element_type=F32)
    h = h * jnp.exp(-t_ref[...] * dl_ref[...])
    row = lax.broadcasted_iota(jnp.int32, h.shape, 0)
    first = (row == 0) & (pl.program_id(0) == 0)
    o_ref[...] = jnp.where(first & (bw_ref[...] > 0.0), 0.0, h).astype(o_ref.dtype)


def _filter_gen(L, C, w1, b1, w2, b2, w3, freq, *, tl, tc):
    emb = w1.shape[0]
    hid = w1.shape[1]
    n_f = w3.shape[1]
    t = np.linspace(0.0, 1.0, L, dtype=np.float32)[:, None]
    w = (2.0 * math.pi * np.arange(L, dtype=np.float32)[:, None] / L).astype(np.float32)
    f = np.linspace(1e-4, FILT_BANDS - 1, FILT_BANDS, dtype=np.float32)[None, :]
    fw = (f * w).astype(np.float32)
    pos = np.zeros((L, 128), np.float32)
    pos[:, :emb] = np.concatenate([t, np.cos(fw), -np.sin(fw)], axis=-1)
    deltas = np.abs(np.linspace(MIN_DECAY, MAX_DECAY, C, dtype=np.float32))
    dl = np.tile(deltas, n_f // C)[None, :]
    bw = np.tile(np.concatenate([np.zeros(C, np.float32), np.ones(C, np.float32)]), n_f // (2 * C))[None, :]
    w1p = jnp.zeros((128, hid), F32).at[:emb].set(w1)
    return pl.pallas_call(
        _filter_kernel,
        grid=(L // tl, n_f // tc),
        in_specs=[
            pl.BlockSpec((tl, 128), lambda i, j: (i, 0)),
            pl.BlockSpec((tl, 1), lambda i, j: (i, 0)),
            pl.BlockSpec((128, hid), lambda i, j: (0, 0)),
            pl.BlockSpec((1, hid), lambda i, j: (0, 0)),
            pl.BlockSpec((hid, hid), lambda i, j: (0, 0)),
            pl.BlockSpec((1, hid), lambda i, j: (0, 0)),
            pl.BlockSpec((hid, tc), lambda i, j: (0, j)),
            pl.BlockSpec((1, hid), lambda i, j: (0, 0)),
            pl.BlockSpec((1, tc), lambda i, j: (0, j)),
            pl.BlockSpec((1, tc), lambda i, j: (0, j)),
        ],
        out_specs=pl.BlockSpec((tl, tc), lambda i, j: (i, j)),
        out_shape=jax.ShapeDtypeStruct((L, n_f), BF16),
        scratch_shapes=[pltpu.VMEM((tl, hid), BF16)],
        compiler_params=_cparams(("parallel", "arbitrary")),
        name="filter_gen",
    )(jnp.asarray(pos), jnp.asarray(t), w1p, b1.reshape(1, hid), w2, b2.reshape(1, hid), w3.astype(BF16),
      freq.reshape(1, hid), jnp.asarray(dl), jnp.asarray(bw))


@functools.lru_cache(maxsize=None)
def _dft_tables(L):
    N = 2 * L
    N1 = FFT_N1
    N2 = N // N1
    H2 = N2 // 2
    k2 = np.arange(H2, dtype=np.float64)
    n2 = np.arange(H2, dtype=np.float64)
    th = 2.0 * np.pi * np.outer(k2 + 0.5, n2) / N2
    ma = np.stack([np.cos(th), -np.sin(th)], axis=1).reshape(N2, H2)
    ga = ma.T * (2.0 / N)
    mak = np.kron(ma, np.eye(KRON_F32))
    gak = np.kron(ga, np.eye(KRON))
    k1 = np.arange(N1, dtype=np.float64)
    n1 = np.arange(N1, dtype=np.float64)
    kk = k1[None, :, None] * N2 + k2[:, None, None] + 0.5
    ph = 2.0 * np.pi * kk * n1[None, None, :] / N
    c, s = np.cos(ph), np.sin(ph)
    fwd = np.concatenate([np.concatenate([c, s], axis=2), np.concatenate([-s, c], axis=2)], axis=1)
    ct, st = np.swapaxes(c, 1, 2), np.swapaxes(s, 1, 2)
    inv = np.concatenate([np.concatenate([ct, -st], axis=2), np.concatenate([st, ct], axis=2)], axis=1)
    return tuple(a.astype(np.float32) for a in (mak, gak, fwd, inv))


def _conv_plan(L):
    H2 = L // FFT_N1
    long_seq = H2 > MAX_K2_PER_STEP
    kb = min(H2, MAX_K2_PER_STEP // 2 if long_seq else MAX_K2_PER_STEP)
    return H2, kb, MXU_LANES, 2 if long_seq else 1


def _mxu_tables(L):
    H2, kb, _, _ = _conv_plan(L)
    mak, gak, fwd, inv = (jnp.asarray(a).astype(BF16) for a in _dft_tables(L))
    return mak.reshape(H2 // kb, 2 * kb * KRON_F32, H2 * KRON_F32), gak, fwd, inv


def _stage_a(z_ref, ma_ref, a_ref):
    h2, tc = z_ref.shape[1], z_ref.shape[3]
    kf = KRON_F32
    for gp in range(FFT_N1 // KRON):
        rows16 = z_ref[0, :, gp * KRON:(gp + 1) * KRON, :].astype(F32)
        parts = []
        for g in range(KRON // kf):
            slab = rows16[:, g * kf:(g + 1) * kf, :].reshape(h2 * kf, tc).astype(BF16)
            r = jnp.dot(ma_ref[0], slab, preferred_element_type=F32)
            parts.append(r.reshape(-1, kf, tc))
        a_ref[:, gp * KRON:(gp + 1) * KRON, :] = jnp.concatenate(parts, axis=1).astype(BF16)


def _k2_loop(a_ref, f_ref, g_ref, k_ref, dst_ref, kb):
    n1 = FFT_N1
    tc = a_ref.shape[-1]

    def body(i, carry):
        d = a_ref[pl.ds(2 * i, 2)].reshape(2 * n1, tc)
        x = jnp.dot(f_ref[i], d, preferred_element_type=F32)
        xr, xi = x[:n1], x[n1:]
        kr, ki = k_ref[0, i, 0], k_ref[0, i, 1]
        y = jnp.concatenate([xr * kr - xi * ki, xr * ki + xi * kr], axis=0).astype(BF16)
        c = jnp.dot(g_ref[i], y, preferred_element_type=F32).astype(BF16)
        dst_ref[i] = c.reshape(2, n1, tc)
        return carry

    lax.fori_loop(0, kb, body, 0, unroll=K2_UNROLL)


def _stage_a_inv(t_ref, ga_ref, z_ref, gate_ref, skip_ref, o_ref):
    h2, rows, tc = z_ref.shape
    for g in range(rows // KRON):
        sl = slice(g * KRON, (g + 1) * KRON)
        slab = t_ref[:, :, sl, :].reshape(2 * h2 * KRON, tc)
        y = jnp.dot(ga_ref[...], slab, preferred_element_type=F32).reshape(h2, KRON, tc)
        out = gate_ref[:, sl, :].astype(F32) * (y + skip_ref[...] * z_ref[:, sl, :].astype(F32))
        o_ref[:, sl, :] = out.astype(o_ref.dtype)


def _conv_fwd_kernel(z_ref, ma_ref, f_ref, g_ref, k_ref, o_ref, a_ref, *, kb):
    _stage_a(z_ref.at[0], ma_ref, a_ref)
    _k2_loop(a_ref, f_ref, g_ref, k_ref, o_ref.at[0], kb)


def _conv_fused_kernel(z_ref, gate_ref, ma_ref, f_ref, g_ref, k_ref, ga_ref, skip_ref, o_ref, a_ref, c_ref, *, kb):
    _stage_a(z_ref.at[0], ma_ref, a_ref)
    _k2_loop(a_ref, f_ref, g_ref, k_ref, c_ref, kb)
    _stage_a_inv(c_ref, ga_ref, z_ref.at[0, 0], gate_ref.at[0, 0], skip_ref, o_ref.at[0, 0])


def _conv_fused(z5, zcomp, gate5, gcomp, mak, fwd, inv, gak, khat, order, skip_row, out_dtype):
    _, B, H2, N1, C = z5.shape
    _, kb, tc, _ = _conv_plan(H2 * N1)
    kern = functools.partial(_conv_fused_kernel, kb=kb)
    tile = lambda comp: pl.BlockSpec((1, 1, H2, N1, tc), lambda c, b: (comp, b, 0, 0, c))
    return pl.pallas_call(
        kern,
        grid=(C // tc, B),
        in_specs=[
            tile(zcomp),
            tile(gcomp),
            pl.BlockSpec((1, 2 * kb * KRON_F32, H2 * KRON_F32), lambda c, b: (0, 0, 0)),
            pl.BlockSpec((kb, 2 * N1, 2 * N1), lambda c, b: (0, 0, 0)),
            pl.BlockSpec((kb, 2 * N1, 2 * N1), lambda c, b: (0, 0, 0)),
            pl.BlockSpec((1, kb, 2, N1, tc), lambda c, b: (order, 0, 0, 0, c)),
            pl.BlockSpec((H2 * KRON, 2 * H2 * KRON), lambda c, b: (0, 0)),
            pl.BlockSpec((1, tc), lambda c, b: (0, c)),
        ],
        out_specs=tile(0),
        out_shape=jax.ShapeDtypeStruct((1, B, H2, N1, C), out_dtype),
        scratch_shapes=[pltpu.VMEM((2 * kb, N1, tc), BF16), pltpu.VMEM((kb, 2, N1, tc), BF16)],
        compiler_params=_cparams(("parallel", "parallel")),
        name="conv_fused",
    )(z5, gate5, mak, fwd, inv, khat, gak, skip_row)


def _conv_fwd(z5, comp, mak, fwd, inv, khat, order):
    _, B, H2, N1, C = z5.shape
    _, kb, tc, _ = _conv_plan(H2 * N1)
    kern = functools.partial(_conv_fwd_kernel, kb=kb)
    return pl.pallas_call(
        kern,
        grid=(C // tc, B, H2 // kb),
        in_specs=[
            pl.BlockSpec((1, 1, H2, N1, tc), lambda c, b, k: (comp, b, 0, 0, c)),
            pl.BlockSpec((1, 2 * kb * KRON_F32, H2 * KRON_F32), lambda c, b, k: (k, 0, 0)),
            pl.BlockSpec((kb, 2 * N1, 2 * N1), lambda c, b, k: (k, 0, 0)),
            pl.BlockSpec((kb, 2 * N1, 2 * N1), lambda c, b, k: (k, 0, 0)),
            pl.BlockSpec((1, kb, 2, N1, tc), lambda c, b, k: (order, k, 0, 0, c)),
        ],
        out_specs=pl.BlockSpec((1, kb, 2, N1, tc), lambda c, b, k: (b, k, 0, 0, c)),
        out_shape=jax.ShapeDtypeStruct((B, H2, 2, N1, C), BF16),
        scratch_shapes=[pltpu.VMEM((2 * kb, N1, tc), BF16)],
        compiler_params=_cparams(("parallel", "parallel", "arbitrary")),
        name="conv_fwd",
    )(z5, mak, fwd, inv, khat)


def _conv_inv_kernel(t_ref, ga_ref, z_ref, gate_ref, skip_ref, o_ref):
    _stage_a_inv(t_ref.at[0], ga_ref, z_ref.at[0, 0], gate_ref.at[0, 0], skip_ref, o_ref.at[0, 0])


def _conv_inv(t2, gak, z5, zcomp, gate5, gcomp, skip_row, out_dtype):
    B, H2, _, N1, C = t2.shape
    _, _, tc, ns = _conv_plan(H2 * N1)
    return pl.pallas_call(
        _conv_inv_kernel,
        grid=(C // tc, B, ns),
        in_specs=[
            pl.BlockSpec((1, H2, 2, N1 // ns, tc), lambda c, b, s: (b, 0, 0, s, c)),
            pl.BlockSpec((H2 * KRON, 2 * H2 * KRON), lambda c, b, s: (0, 0)),
            pl.BlockSpec((1, 1, H2, N1 // ns, tc), lambda c, b, s: (zcomp, b, 0, s, c)),
            pl.BlockSpec((1, 1, H2, N1 // ns, tc), lambda c, b, s: (gcomp, b, 0, s, c)),
            pl.BlockSpec((1, tc), lambda c, b, s: (0, c)),
        ],
        out_specs=pl.BlockSpec((1, 1, H2, N1 // ns, tc), lambda c, b, s: (0, b, 0, s, c)),
        out_shape=jax.ShapeDtypeStruct((1, B, H2, N1, C), out_dtype),
        compiler_params=_cparams(("parallel", "parallel", "parallel")),
        name="conv_inv",
    )(t2, gak, z5, gate5, skip_row)


def _filter_fwd_kernel(hf_ref, hb_ref, ma_ref, f_ref, o_ref, af_ref, ab_ref, *, kb):
    n1 = FFT_N1
    tc = hf_ref.shape[-1]
    _stage_a(hf_ref, ma_ref, af_ref)
    _stage_a(hb_ref, ma_ref, ab_ref)

    def body(i, carry):
        xf = jnp.dot(f_ref[i], af_ref[pl.ds(2 * i, 2)].reshape(2 * n1, tc), preferred_element_type=F32)
        xb = jnp.dot(f_ref[i], ab_ref[pl.ds(2 * i, 2)].reshape(2 * n1, tc), preferred_element_type=F32)
        o_ref[0, i, 0] = xf[:n1] + xb[:n1]
        o_ref[0, i, 1] = xf[n1:] - xb[n1:]
        return carry

    lax.fori_loop(0, kb, body, 0, unroll=K2_UNROLL)


def _filter_fwd(h4, mak, fwd, C):
    _, H2, N1, n_f = h4.shape
    _, kb, tc, _ = _conv_plan(H2 * N1)
    n_order = n_f // (2 * C)
    cpb = C // tc
    kern = functools.partial(_filter_fwd_kernel, kb=kb)
    return pl.pallas_call(
        kern,
        grid=(n_order, cpb, H2 // kb),
        in_specs=[
            pl.BlockSpec((1, H2, N1, tc), lambda o, c, k: (0, 0, 0, (2 * o) * cpb + c)),
            pl.BlockSpec((1, H2, N1, tc), lambda o, c, k: (0, 0, 0, (2 * o + 1) * cpb + c)),
            pl.BlockSpec((1, 2 * kb * KRON_F32, H2 * KRON_F32), lambda o, c, k: (k, 0, 0)),
            pl.BlockSpec((kb, 2 * N1, 2 * N1), lambda o, c, k: (k, 0, 0)),
        ],
        out_specs=pl.BlockSpec((1, kb, 2, N1, tc), lambda o, c, k: (o, k, 0, 0, c)),
        out_shape=jax.ShapeDtypeStruct((n_order, H2, 2, N1, C), F32),
        scratch_shapes=[pltpu.VMEM((2 * kb, N1, tc), BF16), pltpu.VMEM((2 * kb, N1, tc), BF16)],
        compiler_params=_cparams(("parallel", "parallel", "arbitrary")),
        name="filter_fwd",
    )(h4, h4, mak, fwd)


def _filter_spectra(L, C, w1, b1, w2, b2, w3, freq):
    N1 = FFT_N1
    H2 = L // N1
    mak, _, fwd, _ = _mxu_tables(L)
    n_f = w3.shape[1]
    h = _filter_gen(L, C, w1, b1, w2, b2, w3, freq, tl=min(1024, L), tc=min(2048, n_f))
    return _filter_fwd(h.reshape(1, H2, N1, n_f), mak, fwd, C)


def _hyena_mix(u, khat, skip):
    _, B, L, C = u.shape
    N1 = FFT_N1
    H2 = L // N1
    mak, gak, fwd, inv = _mxu_tables(L)
    u5 = u.reshape(3, B, H2, N1, C)
    z5, zc = u5, 0
    _, kb, _, _ = _conv_plan(L)
    for o in range(HYENA_ORDER):
        skip_row = skip[o].astype(F32).reshape(1, C)
        if kb == H2:
            z5 = _conv_fused(z5, zc, u5, o + 1, mak, fwd, inv, gak, khat, o, skip_row, BF16)
        else:
            t2 = _conv_fwd(z5, zc, mak, fwd, inv, khat, o)
            z5 = _conv_inv(t2, gak, z5, zc, u5, o + 1, skip_row, BF16)
        zc = 0
    return z5.reshape(B, L, C)


@functools.lru_cache(maxsize=None)
def _alibi_bias():
    q = np.arange(WINDOW)[:, None]
    kpos = np.arange(3 * WINDOW)[None, :] - WINDOW
    dist = np.abs(q - kpos).astype(np.float32)
    slopes = np.exp2(-8.0 * np.arange(1, N_HEADS + 1, dtype=np.float32) / N_HEADS)
    bias = -slopes[:, None, None] * dist[None]
    bias = np.where(dist[None] <= WINDOW, bias, NEG_INF).astype(np.float32)
    return bias


def _row_sumsq(x):
    sq = x * x
    hi = sq.astype(BF16)
    lo = (sq - hi.astype(F32)).astype(BF16)
    ones = jnp.ones((x.shape[-1], x.shape[-1]), BF16)
    return (jnp.dot(hi, ones, preferred_element_type=F32) + jnp.dot(lo, ones, preferred_element_type=F32))


def _head_norm(x, g):
    return x * lax.rsqrt(_row_sumsq(x) * (1.0 / x.shape[-1]) + NORM_EPS) * g


def _attention_kernel(q_ref, kp_ref, kc_ref, kn_ref, vp_ref, vc_ref, vn_ref, qg_ref, kg_ref,
                      bias_ref, sink_ref, o_ref, *, n_blocks, nq):
    i = pl.program_id(1)
    hd, blk, rep = HEAD_DIM, WINDOW, HEAD_REP
    kk = jnp.concatenate([kp_ref[0], kc_ref[0], kn_ref[0]], axis=0)
    vv = jnp.concatenate([vp_ref[0], vc_ref[0], vn_ref[0]], axis=0)
    col = lax.broadcasted_iota(jnp.int32, (1, 3 * blk), 1)
    for g in range(N_KV_HEADS):
        kn_all = _head_norm(kk[:, g * hd:(g + 1) * hd].astype(F32), kg_ref[...]).astype(BF16)
        vg_all = jnp.concatenate([vv[:, g * hd:(g + 1) * hd].astype(BF16), jnp.ones((kk.shape[0], hd), BF16)], axis=1)
        sink = sink_ref[g * rep:(g + 1) * rep]
        for t in range(nq):
            gi = i * nq + t
            in_seq = ((col >= blk) | (gi > 0)) & ((col < 2 * blk) | (gi < n_blocks - 1))
            rows = slice(t * blk, (t + 1) * blk)
            kn = kn_all[t * blk:(t + 3) * blk]
            vg = vg_all[t * blk:(t + 3) * blk]
            qs = jnp.concatenate(
                [q_ref[0, rows, (g * rep + r) * hd:(g * rep + r + 1) * hd] for r in range(rep)], axis=0)
            qn = _head_norm(qs.astype(F32), qg_ref[...]).astype(BF16)
            s = lax.dot_general(qn, kn, (((1,), (1,)), ((), ())), preferred_element_type=F32)
            s = s.reshape(rep, blk, 3 * blk) * (hd ** -0.5) + bias_ref[g * rep:(g + 1) * rep]
            s = jnp.where(in_seq, s, NEG_INF)
            m = jnp.maximum(jnp.max(s.astype(BF16), axis=-1, keepdims=True).astype(F32), sink)
            p = jnp.exp(s - m)
            pv = jnp.dot(p.reshape(rep * blk, 3 * blk).astype(BF16), vg, preferred_element_type=F32)
            pv = pv.reshape(rep, blk, 2 * hd)
            denom = pv[:, :, hd:] + jnp.exp(sink - m)
            o = pv[:, :, :hd] / denom
            for r in range(rep):
                o_ref[0, rows, (g * rep + r) * hd:(g * rep + r + 1) * hd] = o[r].astype(o_ref.dtype)


def _attention(proj, q_g, k_g, sink, *, off_q, off_k, off_v):
    B, L, _ = proj.shape
    blk = WINDOW
    nb = L // blk
    nq = min(ATT_Q_BLOCKS, nb)
    aw = N_HEADS * HEAD_DIM
    kw = N_KV_HEADS * HEAD_DIM
    qb, kcb, vcb = off_q // aw, off_k // kw, off_v // kw
    kern = functools.partial(_attention_kernel, n_blocks=nb, nq=nq)
    prev = lambda c: (lambda b, i: (b, jnp.maximum(i * nq - 1, 0), c))
    cur = lambda c: (lambda b, i: (b, i, c))
    nxt = lambda c: (lambda b, i: (b, jnp.minimum((i + 1) * nq, nb - 1), c))
    return pl.pallas_call(
        kern,
        grid=(B, nb // nq),
        in_specs=[
            pl.BlockSpec((1, nq * blk, aw), cur(qb)),
            pl.BlockSpec((1, blk, kw), prev(kcb)),
            pl.BlockSpec((1, nq * blk, kw), cur(kcb)),
            pl.BlockSpec((1, blk, kw), nxt(kcb)),
            pl.BlockSpec((1, blk, kw), prev(vcb)),
            pl.BlockSpec((1, nq * blk, kw), cur(vcb)),
            pl.BlockSpec((1, blk, kw), nxt(vcb)),
            pl.BlockSpec((1, HEAD_DIM), lambda b, i: (0, 0)),
            pl.BlockSpec((1, HEAD_DIM), lambda b, i: (0, 0)),
            pl.BlockSpec((N_HEADS, blk, 3 * blk), lambda b, i: (0, 0, 0)),
            pl.BlockSpec((N_HEADS, 1, 1), lambda b, i: (0, 0, 0)),
        ],
        out_specs=pl.BlockSpec((1, nq * blk, aw), lambda b, i: (b, i, 0)),
        out_shape=jax.ShapeDtypeStruct((B, L, aw), BF16),
        compiler_params=_cparams(("parallel", "parallel")),
        name="attention",
    )(proj, proj, proj, proj, proj, proj, proj, q_g.reshape(1, -1), k_g.reshape(1, -1),
      _alibi_bias(), sink.astype(F32).reshape(N_HEADS, 1, 1))


def _merge_kernel(z_ref, a_ref, g_ref, x_ref, woh_ref, woa_ref, wout_ref, o_ref, *, tj):
    d = x_ref.shape[1]
    z = z_ref[...]
    a = a_ref[...]
    acc = x_ref[...]
    for j in range(d // tj):
        cols = slice(j * tj, (j + 1) * tj)
        y_hy = jnp.dot(z, woh_ref[:, cols], preferred_element_type=F32)
        y_at = jnp.dot(a, woa_ref[:, cols], preferred_element_type=F32)
        g_hy = jax.nn.sigmoid(g_ref[:, cols].astype(F32))
        g_at = jax.nn.sigmoid(g_ref[:, d + j * tj:d + (j + 1) * tj].astype(F32))
        mixed = (g_hy * y_hy + g_at * y_at).astype(BF16)
        acc = acc + jnp.dot(mixed, wout_ref[cols, :], preferred_element_type=F32)
    o_ref[...] = acc


def _merge(z, a, gates, x, w_oh, w_oa, w_out, *, tm, tj):
    T, D = x.shape
    cz, ca = z.shape[1], a.shape[1]
    kern = functools.partial(_merge_kernel, tj=tj)
    return pl.pallas_call(
        kern,
        grid=(T // tm,),
        in_specs=[
            pl.BlockSpec((tm, cz), lambda i: (i, 0)),
            pl.BlockSpec((tm, ca), lambda i: (i, 0)),
            pl.BlockSpec((tm, 2 * D), lambda i: (i, 0)),
            pl.BlockSpec((tm, D), lambda i: (i, 0)),
            pl.BlockSpec((cz, D), lambda i: (0, 0)),
            pl.BlockSpec((ca, D), lambda i: (0, 0)),
            pl.BlockSpec((D, D), lambda i: (0, 0)),
        ],
        out_specs=pl.BlockSpec((tm, D), lambda i: (i, 0)),
        out_shape=jax.ShapeDtypeStruct((T, D), F32),
        compiler_params=_cparams(("parallel",)),
        name="merge",
    )(z, a, gates, x, w_oh, w_oa, w_out)


def _mlp_kernel(x_ref, g_ref, wu_ref, wd_ref, o_ref, xn_ref):
    @pl.when(pl.program_id(1) == 0)
    def _():
        x = x_ref[...]
        ms = jnp.mean(x * x, axis=-1, keepdims=True)
        xn_ref[...] = (x * lax.rsqrt(ms + NORM_EPS) * g_ref[...]).astype(BF16)
        o_ref[...] = x

    h = jnp.dot(xn_ref[...], wu_ref[...], preferred_element_type=F32)
    h = jnp.square(jnp.maximum(h, 0.0)).astype(BF16)
    o_ref[...] += jnp.dot(h, wd_ref[...], preferred_element_type=F32)


def _mlp(x, g, w_up, w_down, *, tm, tf):
    T, D = x.shape
    d_ff = w_up.shape[1]
    return pl.pallas_call(
        _mlp_kernel,
        grid=(T // tm, d_ff // tf),
        in_specs=[
            pl.BlockSpec((tm, D), lambda i, j: (i, 0)),
            pl.BlockSpec((1, D), lambda i, j: (0, 0)),
            pl.BlockSpec((D, tf), lambda i, j: (0, j)),
            pl.BlockSpec((tf, D), lambda i, j: (j, 0)),
        ],
        out_specs=pl.BlockSpec((tm, D), lambda i, j: (i, 0)),
        out_shape=jax.ShapeDtypeStruct((T, D), F32),
        scratch_shapes=[pltpu.VMEM((tm, D), BF16)],
        compiler_params=_cparams(("parallel", "arbitrary")),
        name="mlp",
    )(x, g.reshape(1, D), w_up, w_down)


def _row_tile(T, pref):
    return pref if T % pref == 0 else T


def _encoder_layer(x, p, khat):
    B, L, D = x.shape
    T = B * L
    C = p["hyena_skip"].shape[1]
    off_q = 2 * D
    off_k = off_q + N_HEADS * HEAD_DIM
    off_v = off_k + N_KV_HEADS * HEAD_DIM
    x2 = x.reshape(T, D)
    tm = _row_tile(L, 1024)
    u = _in_proj_hyena(x2, p["norm_mix_g"], p["w_in_hyena"], p["conv_w"], p["conv_b"], B=B, L=L, C=C, tm=tm, tn=512)
    rest = _norm_matmul(x2, p["norm_mix_g"], p["w_in_rest"], tm=_row_tile(T, 512), tn=512, name="in_proj_rest",
                        out_dtype=BF16)
    z = _hyena_mix(u, khat, p["hyena_skip"])
    att = _attention(rest.reshape(B, L, -1), p["q_norm_g"], p["k_norm_g"], p["attn_sink"],
                     off_q=off_q, off_k=off_k, off_v=off_v)
    x1 = _merge(z.reshape(T, C), att.reshape(T, -1), rest, x2, p["w_o_hyena"], p["w_o_attn"],
                p["w_out"], tm=_row_tile(T, 512), tj=512)
    y = _mlp(x1, p["norm_mlp_g"], p["w_up"], p["w_down"], tm=tm, tf=512)
    return y.reshape(B, L, D)


def kernel(x_prompt, x_sample, norm_mix_g, w_in, conv_w, conv_b, filt_w1, filt_b1, filt_w2, filt_b2,
           filt_w3, filt_freq, hyena_skip, q_norm_g, k_norm_g, attn_sink, w_o_hyena, w_o_attn, w_out,
           norm_mlp_g, w_up, w_down):
    depth = w_in.shape[0]
    D = x_prompt.shape[-1]
    C = hyena_skip.shape[2]
    off_g = w_in.shape[2] - 2 * D
    off_q = (HYENA_ORDER + 1) * C
    y_prompt, y_sample = x_prompt, x_sample
    for l in range(depth):
        p = dict(
            norm_mix_g=norm_mix_g[l], w_in_hyena=w_in[l, :, :off_q].astype(BF16),
            w_in_rest=jnp.concatenate([w_in[l, :, off_g:], w_in[l, :, off_q:off_g]], axis=1).astype(BF16),
            conv_w=conv_w[l], conv_b=conv_b[l],
            hyena_skip=hyena_skip[l], q_norm_g=q_norm_g[l], k_norm_g=k_norm_g[l], attn_sink=attn_sink[l],
            w_o_hyena=w_o_hyena[l].astype(BF16), w_o_attn=w_o_attn[l].astype(BF16),
            w_out=w_out[l].astype(BF16), norm_mlp_g=norm_mlp_g[l],
            w_up=w_up[l].astype(BF16), w_down=w_down[l].astype(BF16),
        )
        filt = (filt_w1[l], filt_b1[l], filt_w2[l], filt_b2[l], filt_w3[l], filt_freq[l])
        outs = []
        for x in (y_prompt, y_sample):
            khat = _filter_spectra(x.shape[1], C, *filt)
            outs.append(_encoder_layer(x, p, khat))
        y_prompt, y_sample = outs
    return (y_prompt, y_sample)
```

```python
import functools
import math

import numpy as np
import jax
import jax.numpy as jnp
from jax import lax
from jax.experimental import pallas as pl
from jax.experimental.pallas import tpu as pltpu

F32 = jnp.float32
BF16 = jnp.bfloat16

NORM_EPS = 1e-6
NEG_INF = -1e30
LOG2E = 1.4426950408889634

HYENA_ORDER = 2
N_HEADS = 8
N_KV_HEADS = 2
HEAD_REP = N_HEADS // N_KV_HEADS
HEAD_DIM = 128
WINDOW = 128
ATT_Q_BLOCKS = 8
FILT_BANDS = 16
DECAY_TARGET = 1e-2
MAX_DECAY = math.log(DECAY_TARGET) / 0.3
MIN_DECAY = math.log(DECAY_TARGET) / 1.5

FFT_N1 = 128
MXU_LANES = 256
KRON_F32 = 8
KRON = 16
MAX_K2_PER_STEP = 32
K2_UNROLL = 16
VMEM_LIMIT = 56 * 1024 * 1024


def _cparams(sem):
    return pltpu.CompilerParams(dimension_semantics=sem, vmem_limit_bytes=VMEM_LIMIT)


def _rms_rows(x, g):
    ms = jnp.mean(x * x, axis=-1, keepdims=True)
    return (x * lax.rsqrt(ms + NORM_EPS) * g).astype(BF16)


def _norm_matmul_kernel(x_ref, g_ref, w_ref, o_ref, *, tn):
    xn = _rms_rows(x_ref[...], g_ref[...])
    for j in range(w_ref.shape[1] // tn):
        cols = slice(j * tn, (j + 1) * tn)
        o_ref[:, cols] = jnp.dot(xn, w_ref[:, cols], preferred_element_type=F32).astype(o_ref.dtype)


def _norm_matmul(x, g, w, *, tm, tn, name, out_dtype):
    T, D = x.shape
    n_out = w.shape[1]
    return pl.pallas_call(
        functools.partial(_norm_matmul_kernel, tn=tn),
        grid=(T // tm,),
        in_specs=[
            pl.BlockSpec((tm, D), lambda i: (i, 0)),
            pl.BlockSpec((1, D), lambda i: (0, 0)),
            pl.BlockSpec((D, n_out), lambda i: (0, 0)),
        ],
        out_specs=pl.BlockSpec((tm, n_out), lambda i: (i, 0)),
        out_shape=jax.ShapeDtypeStruct((T, n_out), out_dtype),
        compiler_params=_cparams(("parallel",)),
        name=name,
    )(x, g.reshape(1, D), w)


HALO = 16


def _in_proj_hyena_kernel(x_ref, xp_ref, xn_ref, g_ref, w_ref, cw_ref, cb_ref, o_ref, xs_ref, *, blocks_per_seq, tn):
    tm = x_ref.shape[0]
    c = o_ref.shape[-1]
    r = pl.program_id(0) % blocks_per_seq
    g = g_ref[...]
    xs_ref[0:HALO, :] = jnp.where(r > 0, _rms_rows(xp_ref[...], g), jnp.zeros((), BF16))
    xs_ref[HALO:HALO + tm, :] = _rms_rows(x_ref[...], g)
    xs_ref[HALO + tm:, :] = jnp.where(r < blocks_per_seq - 1, _rms_rows(xn_ref[...], g), jnp.zeros((), BF16))
    rows = tm + 2 * HALO
    for j in range(w_ref.shape[1] // tn):
        cols = slice(j * tn, (j + 1) * tn)
        p = jnp.dot(xs_ref[...], w_ref[:, cols], preferred_element_type=F32)
        prev = pltpu.roll(p, 1, axis=0)[HALO:HALO + tm]
        nxt = pltpu.roll(p, rows - 1, axis=0)[HALO:HALO + tm]
        u = prev * cw_ref[0:1, cols] + p[HALO:HALO + tm] * cw_ref[1:2, cols] + nxt * cw_ref[2:3, cols] + cb_ref[:, cols]
        comp, c0 = divmod(j * tn, c)
        o_ref[comp, 0, :, c0:c0 + tn] = u.astype(o_ref.dtype)


def _in_proj_hyena(x, g, w, conv_w, conv_b, *, B, L, C, tm, tn):
    T, D = x.shape
    bps = L // tm
    hpb = tm // HALO
    n_out = w.shape[1]
    kern = functools.partial(_in_proj_hyena_kernel, blocks_per_seq=bps, tn=tn)
    return pl.pallas_call(
        kern,
        grid=(T // tm,),
        in_specs=[
            pl.BlockSpec((tm, D), lambda i: (i, 0)),
            pl.BlockSpec((HALO, D), lambda i: (jnp.maximum(i * hpb - 1, 0), 0)),
            pl.BlockSpec((HALO, D), lambda i: (jnp.minimum((i + 1) * hpb, T // HALO - 1), 0)),
            pl.BlockSpec((1, D), lambda i: (0, 0)),
            pl.BlockSpec((D, n_out), lambda i: (0, 0)),
            pl.BlockSpec((3, n_out), lambda i: (0, 0)),
            pl.BlockSpec((1, n_out), lambda i: (0, 0)),
        ],
        out_specs=pl.BlockSpec((n_out // C, 1, tm, C), lambda i: (0, i // bps, i % bps, 0)),
        out_shape=jax.ShapeDtypeStruct((n_out // C, B, L, C), BF16),
        scratch_shapes=[pltpu.VMEM((tm + 2 * HALO, D), BF16)],
        compiler_params=_cparams(("parallel",)),
        name="in_proj_hyena",
    )(x, x, x, g.reshape(1, D), w, conv_w, conv_b.reshape(1, -1))


def _filter_kernel(pos_ref, t_ref, w1_ref, b1_ref, w2_ref, b2_ref, w3_ref, fr_ref,
                   dl_ref, bw_ref, o_ref, h_ref):
    hi = lax.Precision.HIGHEST

    @pl.when(pl.program_id(1) == 0)
    def _():
        fr = fr_ref[...]
        a = jnp.dot(pos_ref[...], w1_ref[...], precision=hi, preferred_element_type=F32) + b1_ref[...]
        a = jnp.sin(fr * a)
        a = jnp.dot(a, w2_ref[...], precision=hi, preferred_element_type=F32) + b2_ref[...]
        h_ref[...] = jnp.sin(fr * a).astype(BF16)

    h = jnp.dot(h_ref[...], w3_ref[...], preferred_element_type=F32)
    h = h * jnp.exp(-t_ref[...] * dl_ref[...])
    row = lax.broadcasted_iota(jnp.int32, h.shape, 0)
    first = (row == 0) & (pl.program_id(0) == 0)
    o_ref[...] = jnp.where(first & (bw_ref[...] > 0.0), 0.0, h).astype(o_ref.dtype)


def _filter_gen(L, C, w1, b1, w2, b2, w3, freq, *, tl, tc):
    emb = w1.shape[0]
    hid = w1.shape[1]
    n_f = w3.shape[1]
    t = np.linspace(0.0, 1.0, L, dtype=np.float32)[:, None]
    w = (2.0 * math.pi * np.arange(L, dtype=np.float32)[:, None] / L).astype(np.float32)
    f = np.linspace(1e-4, FILT_BANDS - 1, FILT_BANDS, dtype=np.float32)[None, :]
    fw = (f * w).astype(np.float32)
    pos = np.zeros((L, 128), np.float32)
    pos[:, :emb] = np.concatenate([t, np.cos(fw), -np.sin(fw)], axis=-1)
    deltas = np.abs(np.linspace(MIN_DECAY, MAX_DECAY, C, dtype=np.float32))
    dl = np.tile(deltas, n_f // C)[None, :]
    bw = np.tile(np.concatenate([np.zeros(C, np.float32), np.ones(C, np.float32)]), n_f // (2 * C))[None, :]
    w1p = jnp.zeros((128, hid), F32).at[:emb].set(w1)
    return pl.pallas_call(
        _filter_kernel,
        grid=(L // tl, n_f // tc),
        in_specs=[
            pl.BlockSpec((tl, 128), lambda i, j: (i, 0)),
            pl.BlockSpec((tl, 1), lambda i, j: (i, 0)),
            pl.BlockSpec((128, hid), lambda i, j: (0, 0)),
            pl.BlockSpec((1, hid), lambda i, j: (0, 0)),
            pl.BlockSpec((hid, hid), lambda i, j: (0, 0)),
            pl.BlockSpec((1, hid), lambda i, j: (0, 0)),
            pl.BlockSpec((hid, tc), lambda i, j: (0, j)),
            pl.BlockSpec((1, hid), lambda i, j: (0, 0)),
            pl.BlockSpec((1, tc), lambda i, j: (0, j)),
            pl.BlockSpec((1, tc), lambda i, j: (0, j)),
        ],
        out_specs=pl.BlockSpec((tl, tc), lambda i, j: (i, j)),
        out_shape=jax.ShapeDtypeStruct((L, n_f), BF16),
        scratch_shapes=[pltpu.VMEM((tl, hid), BF16)],
        compiler_params=_cparams(("parallel", "arbitrary")),
        name="filter_gen",
    )(jnp.asarray(pos), jnp.asarray(t), w1p, b1.reshape(1, hid), w2, b2.reshape(1, hid), w3.astype(BF16),
      freq.reshape(1, hid), jnp.asarray(dl), jnp.asarray(bw))


@functools.lru_cache(maxsize=None)
def _dft_tables(L):
    N = 2 * L
    N1 = FFT_N1
    N2 = N // N1
    H2 = N2 // 2
    k2 = np.arange(H2, dtype=np.float64)
    n2 = np.arange(H2, dtype=np.float64)
    th = 2.0 * np.pi * np.outer(k2 + 0.5, n2) / N2
    ma = np.stack([np.cos(th), -np.sin(th)], axis=1).reshape(N2, H2)
    ga = ma.T * (2.0 / N)
    mak = np.kron(ma, np.eye(KRON_F32))
    gak = np.kron(ga, np.eye(KRON))
    k1 = np.arange(N1, dtype=np.float64)
    n1 = np.arange(N1, dtype=np.float64)
    kk = k1[None, :, None] * N2 + k2[:, None, None] + 0.5
    ph = 2.0 * np.pi * kk * n1[None, None, :] / N
    c, s = np.cos(ph), np.sin(ph)
    fwd = np.concatenate([np.concatenate([c, s], axis=2), np.concatenate([-s, c], axis=2)], axis=1)
    ct, st = np.swapaxes(c, 1, 2), np.swapaxes(s, 1, 2)
    inv = np.concatenate([np.concatenate([ct, -st], axis=2), np.concatenate([st, ct], axis=2)], axis=1)
    return tuple(a.astype(np.float32) for a in (mak, gak, fwd, inv))


def _conv_plan(L):
    H2 = L // FFT_N1
    long_seq = H2 > MAX_K2_PER_STEP
    kb = min(H2, MAX_K2_PER_STEP // 2 if long_seq else MAX_K2_PER_STEP)
    return H2, kb, MXU_LANES, 2 if long_seq else 1


def _mxu_tables(L):
    H2, kb, _, _ = _conv_plan(L)
    mak, gak, fwd, inv = (jnp.asarray(a).astype(BF16) for a in _dft_tables(L))
    return mak.reshape(H2 // kb, 2 * kb * KRON_F32, H2 * KRON_F32), gak, fwd, inv


def _stage_a(z_ref, ma_ref, a_ref):
    h2, tc = z_ref.shape[1], z_ref.shape[3]
    kf = KRON_F32
    for gp in range(FFT_N1 // KRON):
        rows16 = z_ref[0, :, gp * KRON:(gp + 1) * KRON, :].astype(F32)
        parts = []
        for g in range(KRON // kf):
            slab = rows16[:, g * kf:(g + 1) * kf, :].reshape(h2 * kf, tc).astype(BF16)
            r = jnp.dot(ma_ref[0], slab, preferred_element_type=F32)
            parts.append(r.reshape(-1, kf, tc))
        a_ref[:, gp * KRON:(gp + 1) * KRON, :] = jnp.concatenate(parts, axis=1).astype(BF16)


def _k2_loop(a_ref, f_ref, g_ref, k_ref, dst_ref, kb):
    n1 = FFT_N1
    tc = a_ref.shape[-1]

    def body(i, carry):
        d = a_ref[pl.ds(2 * i, 2)].reshape(2 * n1, tc)
        x = jnp.dot(f_ref[i], d, preferred_element_type=F32)
        xr, xi = x[:n1], x[n1:]
        kr, ki = k_ref[0, i, 0], k_ref[0, i, 1]
        y = jnp.concatenate([xr * kr - xi * ki, xr * ki + xi * kr], axis=0).astype(BF16)
        c = jnp.dot(g_ref[i], y, preferred_element_type=F32).astype(BF16)
        dst_ref[i] = c.reshape(2, n1, tc)
        return carry

    lax.fori_loop(0, kb, body, 0, unroll=K2_UNROLL)


def _stage_a_inv(t_ref, ga_ref, z_ref, gate_ref, skip_ref, o_ref):
    h2, rows, tc = z_ref.shape
    for g in range(rows // KRON):
        sl = slice(g * KRON, (g + 1) * KRON)
        slab = t_ref[:, :, sl, :].reshape(2 * h2 * KRON, tc)
        y = jnp.dot(ga_ref[...], slab, preferred_element_type=F32).reshape(h2, KRON, tc)
        out = gate_ref[:, sl, :].astype(F32) * (y + skip_ref[...] * z_ref[:, sl, :].astype(F32))
        o_ref[:, sl, :] = out.astype(o_ref.dtype)


def _conv_fwd_kernel(z_ref, ma_ref, f_ref, g_ref, k_ref, o_ref, a_ref, *, kb):
    _stage_a(z_ref.at[0], ma_ref, a_ref)
    _k2_loop(a_ref, f_ref, g_ref, k_ref, o_ref.at[0], kb)


def _conv_fused_kernel(z_ref, gate_ref, ma_ref, f_ref, g_ref, k_ref, ga_ref, skip_ref, o_ref, a_ref, c_ref, *, kb):
    _stage_a(z_ref.at[0], ma_ref, a_ref)
    _k2_loop(a_ref, f_ref, g_ref, k_ref, c_ref, kb)
    _stage_a_inv(c_ref, ga_ref, z_ref.at[0, 0], gate_ref.at[0, 0], skip_ref, o_ref.at[0, 0])


def _conv_fused(z5, zcomp, gate5, gcomp, mak, fwd, inv, gak, khat, order, skip_row, out_dtype):
    _, B, H2, N1, C = z5.shape
    _, kb, tc, _ = _conv_plan(H2 * N1)
    kern = functools.partial(_conv_fused_kernel, kb=kb)
    tile = lambda comp: pl.BlockSpec((1, 1, H2, N1, tc), lambda c, b: (comp, b, 0, 0, c))
    return pl.pallas_call(
        kern,
        grid=(C // tc, B),
        in_specs=[
            tile(zcomp),
            tile(gcomp),
            pl.BlockSpec((1, 2 * kb * KRON_F32, H2 * KRON_F32), lambda c, b: (0, 0, 0)),
            pl.BlockSpec((kb, 2 * N1, 2 * N1), lambda c, b: (0, 0, 0)),
            pl.BlockSpec((kb, 2 * N1, 2 * N1), lambda c, b: (0, 0, 0)),
            pl.BlockSpec((1, kb, 2, N1, tc), lambda c, b: (order, 0, 0, 0, c)),
            pl.BlockSpec((H2 * KRON, 2 * H2 * KRON), lambda c, b: (0, 0)),
            pl.BlockSpec((1, tc), lambda c, b: (0, c)),
        ],
        out_specs=tile(0),
        out_shape=jax.ShapeDtypeStruct((1, B, H2, N1, C), out_dtype),
        scratch_shapes=[pltpu.VMEM((2 * kb, N1, tc), BF16), pltpu.VMEM((kb, 2, N1, tc), BF16)],
        compiler_params=_cparams(("parallel", "parallel")),
        name="conv_fused",
    )(z5, gate5, mak, fwd, inv, khat, gak, skip_row)


def _conv_fwd(z5, comp, mak, fwd, inv, khat, order):
    _, B, H2, N1, C = z5.shape
    _, kb, tc, _ = _conv_plan(H2 * N1)
    kern = functools.partial(_conv_fwd_kernel, kb=kb)
    return pl.pallas_call(
        kern,
        grid=(C // tc, B, H2 // kb),
        in_specs=[
            pl.BlockSpec((1, 1, H2, N1, tc), lambda c, b, k: (comp, b, 0, 0, c)),
            pl.BlockSpec((1, 2 * kb * KRON_F32, H2 * KRON_F32), lambda c, b, k: (k, 0, 0)),
            pl.BlockSpec((kb, 2 * N1, 2 * N1), lambda c, b, k: (k, 0, 0)),
            pl.BlockSpec((kb, 2 * N1, 2 * N1), lambda c, b, k: (k, 0, 0)),
            pl.BlockSpec((1, kb, 2, N1, tc), lambda c, b, k: (order, k, 0, 0, c)),
        ],
        out_specs=pl.BlockSpec((1, kb, 2, N1, tc), lambda c, b, k: (b, k, 0, 0, c)),
        out_shape=jax.ShapeDtypeStruct((B, H2, 2, N1, C), BF16),
        scratch_shapes=[pltpu.VMEM((2 * kb, N1, tc), BF16)],
        compiler_params=_cparams(("parallel", "parallel", "arbitrary")),
        name="conv_fwd",
    )(z5, mak, fwd, inv, khat)


def _conv_inv_kernel(t_ref, ga_ref, z_ref, gate_ref, skip_ref, o_ref):
    _stage_a_inv(t_ref.at[0], ga_ref, z_ref.at[0, 0], gate_ref.at[0, 0], skip_ref, o_ref.at[0, 0])


def _conv_inv(t2, gak, z5, zcomp, gate5, gcomp, skip_row, out_dtype):
    B, H2, _, N1, C = t2.shape
    _, _, tc, ns = _conv_plan(H2 * N1)
    return pl.pallas_call(
        _conv_inv_kernel,
        grid=(C // tc, B, ns),
        in_specs=[
            pl.BlockSpec((1, H2, 2, N1 // ns, tc), lambda c, b, s: (b, 0, 0, s, c)),
            pl.BlockSpec((H2 * KRON, 2 * H2 * KRON), lambda c, b, s: (0, 0)),
            pl.BlockSpec((1, 1, H2, N1 // ns, tc), lambda c, b, s: (zcomp, b, 0, s, c)),
            pl.BlockSpec((1, 1, H2, N1 // ns, tc), lambda c, b, s: (gcomp, b, 0, s, c)),
            pl.BlockSpec((1, tc), lambda c, b, s: (0, c)),
        ],
        out_specs=pl.BlockSpec((1, 1, H2, N1 // ns, tc), lambda c, b, s: (0, b, 0, s, c)),
        out_shape=jax.ShapeDtypeStruct((1, B, H2, N1, C), out_dtype),
        compiler_params=_cparams(("parallel", "parallel", "parallel")),
        name="conv_inv",
    )(t2, gak, z5, gate5, skip_row)


def _filter_fwd_kernel(hf_ref, hb_ref, ma_ref, f_ref, o_ref, af_ref, ab_ref, *, kb):
    n1 = FFT_N1
    tc = hf_ref.shape[-1]
    _stage_a(hf_ref, ma_ref, af_ref)
    _stage_a(hb_ref, ma_ref, ab_ref)

    def body(i, carry):
        xf = jnp.dot(f_ref[i], af_ref[pl.ds(2 * i, 2)].reshape(2 * n1, tc), preferred_element_type=F32)
        xb = jnp.dot(f_ref[i], ab_ref[pl.ds(2 * i, 2)].reshape(2 * n1, tc), preferred_element_type=F32)
        o_ref[0, i, 0] = xf[:n1] + xb[:n1]
        o_ref[0, i, 1] = xf[n1:] - xb[n1:]
        return carry

    lax.fori_loop(0, kb, body, 0, unroll=K2_UNROLL)


def _filter_fwd(h4, mak, fwd, C):
    _, H2, N1, n_f = h4.shape
    _, kb, tc, _ = _conv_plan(H2 * N1)
    n_order = n_f // (2 * C)
    cpb = C // tc
    kern = functools.partial(_filter_fwd_kernel, kb=kb)
    return pl.pallas_call(
        kern,
        grid=(n_order, cpb, H2 // kb),
        in_specs=[
            pl.BlockSpec((1, H2, N1, tc), lambda o, c, k: (0, 0, 0, (2 * o) * cpb + c)),
            pl.BlockSpec((1, H2, N1, tc), lambda o, c, k: (0, 0, 0, (2 * o + 1) * cpb + c)),
            pl.BlockSpec((1, 2 * kb * KRON_F32, H2 * KRON_F32), lambda o, c, k: (k, 0, 0)),
            pl.BlockSpec((kb, 2 * N1, 2 * N1), lambda o, c, k: (k, 0, 0)),
        ],
        out_specs=pl.BlockSpec((1, kb, 2, N1, tc), lambda o, c, k: (o, k, 0, 0, c)),
        out_shape=jax.ShapeDtypeStruct((n_order, H2, 2, N1, C), F32),
        scratch_shapes=[pltpu.VMEM((2 * kb, N1, tc), BF16), pltpu.VMEM((2 * kb, N1, tc), BF16)],
        compiler_params=_cparams(("parallel", "parallel", "arbitrary")),
        name="filter_fwd",
    )(h4, h4, mak, fwd)


def _filter_spectra(L, C, w1, b1, w2, b2, w3, freq):
    N1 = FFT_N1
    H2 = L // N1
    mak, _, fwd, _ = _mxu_tables(L)
    n_f = w3.shape[1]
    h = _filter_gen(L, C, w1, b1, w2, b2, w3, freq, tl=min(1024, L), tc=min(2048, n_f))
    return _filter_fwd(h.reshape(1, H2, N1, n_f), mak, fwd, C)


def _hyena_mix(u, khat, skip):
    _, B, L, C = u.shape
    N1 = FFT_N1
    H2 = L // N1
    mak, gak, fwd, inv = _mxu_tables(L)
    u5 = u.reshape(3, B, H2, N1, C)
    z5, zc = u5, 0
    _, kb, _, _ = _conv_plan(L)
    for o in range(HYENA_ORDER):
        skip_row = skip[o].astype(F32).reshape(1, C)
        if kb == H2:
            z5 = _conv_fused(z5, zc, u5, o + 1, mak, fwd, inv, gak, khat, o, skip_row, BF16)
        else:
            t2 = _conv_fwd(z5, zc, mak, fwd, inv, khat, o)
            z5 = _conv_inv(t2, gak, z5, zc, u5, o + 1, skip_row, BF16)
        zc = 0
    return z5.reshape(B, L, C)


@functools.lru_cache(maxsize=None)
def _alibi_bias():
    q = np.arange(WINDOW)[:, None]
    kpos = np.arange(3 * WINDOW)[None, :] - WINDOW
    dist = np.abs(q - kpos).astype(np.float32)
    slopes = np.exp2(-8.0 * np.arange(1, N_HEADS + 1, dtype=np.float32) / N_HEADS)
    bias = -slopes[:, None, None] * dist[None] * np.float32(LOG2E)
    bias = np.where(dist[None] <= WINDOW, bias, NEG_INF).astype(np.float32)
    return bias.reshape(N_KV_HEADS, HEAD_REP, WINDOW, 3 * WINDOW).transpose(0, 3, 1, 2).reshape(
        N_KV_HEADS, 3 * WINDOW, HEAD_REP * WINDOW).copy()


def _row_sumsq(x):
    sq = x * x
    hi = sq.astype(BF16)
    lo = (sq - hi.astype(F32)).astype(BF16)
    ones = jnp.ones((x.shape[-1], x.shape[-1]), BF16)
    return (jnp.dot(hi, ones, preferred_element_type=F32) + jnp.dot(lo, ones, preferred_element_type=F32))


def _head_norm(x, g):
    return x * lax.rsqrt(_row_sumsq(x) * (1.0 / x.shape[-1]) + NORM_EPS) * g


def _attention_kernel(q_ref, kp_ref, kc_ref, kn_ref, vp_ref, vc_ref, vn_ref, qg_ref, kg_ref,
                      bias_ref, sink_ref, o_ref, *, n_blocks, nq):
    i = pl.program_id(1)
    hd, blk, rep = HEAD_DIM, WINDOW, HEAD_REP
    kk = jnp.concatenate([kp_ref[0], kc_ref[0], kn_ref[0]], axis=0)
    vv = jnp.concatenate([vp_ref[0], vc_ref[0], vn_ref[0]], axis=0)
    krow = lax.broadcasted_iota(jnp.int32, (3 * blk, 1), 0)
    nt = (((1,), (1,)), ((), ()))
    tn = (((0,), (0,)), ((), ()))
    for g in range(N_KV_HEADS):
        kn_all = _head_norm(kk[:, g * hd:(g + 1) * hd].astype(F32), kg_ref[...]).astype(BF16)
        vg_all = vv[:, g * hd:(g + 1) * hd].astype(BF16)
        sink = sink_ref[g]
        for t in range(nq):
            gi = i * nq + t
            in_seq = ((krow >= blk) | (gi > 0)) & ((krow < 2 * blk) | (gi < n_blocks - 1))
            rows = slice(t * blk, (t + 1) * blk)
            kn = kn_all[t * blk:(t + 3) * blk]
            vg = vg_all[t * blk:(t + 3) * blk]
            qs = jnp.concatenate(
                [q_ref[0, rows, (g * rep + r) * hd:(g * rep + r + 1) * hd] for r in range(rep)], axis=0)
            qn = (_head_norm(qs.astype(F32), qg_ref[...]) * (LOG2E * hd ** -0.5)).astype(BF16)
            st = lax.dot_general(kn, qn, nt, preferred_element_type=F32) + bias_ref[g]
            st = jnp.where(in_seq, st, NEG_INF)
            m = jnp.maximum(jnp.max(st, axis=0, keepdims=True), sink)
            p = jnp.exp2(st - m)
            denom = jnp.sum(p, axis=0, keepdims=True) + jnp.exp2(sink - m)
            ot = lax.dot_general(vg, p.astype(BF16), tn, preferred_element_type=F32) / denom
            for r in range(rep):
                o_ref[0, rows, (g * rep + r) * hd:(g * rep + r + 1) * hd] = (
                    ot[:, r * blk:(r + 1) * blk].T.astype(o_ref.dtype))


def _attention(proj, q_g, k_g, sink, *, off_q, off_k, off_v):
    B, L, _ = proj.shape
    blk = WINDOW
    nb = L // blk
    nq = min(ATT_Q_BLOCKS, nb)
    aw = N_HEADS * HEAD_DIM
    kw = N_KV_HEADS * HEAD_DIM
    qb, kcb, vcb = off_q // aw, off_k // kw, off_v // kw
    kern = functools.partial(_attention_kernel, n_blocks=nb, nq=nq)
    prev = lambda c: (lambda b, i: (b, jnp.maximum(i * nq - 1, 0), c))
    cur = lambda c: (lambda b, i: (b, i, c))
    nxt = lambda c: (lambda b, i: (b, jnp.minimum((i + 1) * nq, nb - 1), c))
    return pl.pallas_call(
        kern,
        grid=(B, nb // nq),
        in_specs=[
            pl.BlockSpec((1, nq * blk, aw), cur(qb)),
            pl.BlockSpec((1, blk, kw), prev(kcb)),
            pl.BlockSpec((1, nq * blk, kw), cur(kcb)),
            pl.BlockSpec((1, blk, kw), nxt(kcb)),
            pl.BlockSpec((1, blk, kw), prev(vcb)),
            pl.BlockSpec((1, nq * blk, kw), cur(vcb)),
            pl.BlockSpec((1, blk, kw), nxt(vcb)),
            pl.BlockSpec((1, HEAD_DIM), lambda b, i: (0, 0)),
            pl.BlockSpec((1, HEAD_DIM), lambda b, i: (0, 0)),
            pl.BlockSpec((N_KV_HEADS, 3 * blk, HEAD_REP * blk), lambda b, i: (0, 0, 0)),
            pl.BlockSpec((N_KV_HEADS, 1, HEAD_REP * blk), lambda b, i: (0, 0, 0)),
        ],
        out_specs=pl.BlockSpec((1, nq * blk, aw), lambda b, i: (b, i, 0)),
        out_shape=jax.ShapeDtypeStruct((B, L, aw), BF16),
        compiler_params=_cparams(("parallel", "parallel")),
        name="attention",
    )(proj, proj, proj, proj, proj, proj, proj, q_g.reshape(1, -1), k_g.reshape(1, -1),
      _alibi_bias(), jnp.repeat(sink.astype(F32) * LOG2E, blk).reshape(N_KV_HEADS, 1, HEAD_REP * blk))


def _merge_kernel(z_ref, a_ref, g_ref, x_ref, woh_ref, woa_ref, wout_ref, o_ref, *, tj):
    d = x_ref.shape[1]
    z = z_ref[...]
    a = a_ref[...]
    acc = x_ref[...]
    for j in range(d // tj):
        cols = slice(j * tj, (j + 1) * tj)
        y_hy = jnp.dot(z, woh_ref[:, cols], preferred_element_type=F32)
        y_at = jnp.dot(a, woa_ref[:, cols], preferred_element_type=F32)
        g_hy = jax.nn.sigmoid(g_ref[:, cols].astype(F32))
        g_at = jax.nn.sigmoid(g_ref[:, d + j * tj:d + (j + 1) * tj].astype(F32))
        mixed = (g_hy * y_hy + g_at * y_at).astype(BF16)
        acc = acc + jnp.dot(mixed, wout_ref[cols, :], preferred_element_type=F32)
    o_ref[...] = acc


def _merge(z, a, gates, x, w_oh, w_oa, w_out, *, tm, tj):
    T, D = x.shape
    cz, ca = z.shape[1], a.shape[1]
    kern = functools.partial(_merge_kernel, tj=tj)
    return pl.pallas_call(
        kern,
        grid=(T // tm,),
        in_specs=[
            pl.BlockSpec((tm, cz), lambda i: (i, 0)),
            pl.BlockSpec((tm, ca), lambda i: (i, 0)),
            pl.BlockSpec((tm, 2 * D), lambda i: (i, 0)),
            pl.BlockSpec((tm, D), lambda i: (i, 0)),
            pl.BlockSpec((cz, D), lambda i: (0, 0)),
            pl.BlockSpec((ca, D), lambda i: (0, 0)),
            pl.BlockSpec((D, D), lambda i: (0, 0)),
        ],
        out_specs=pl.BlockSpec((tm, D), lambda i: (i, 0)),
        out_shape=jax.ShapeDtypeStruct((T, D), F32),
        compiler_params=_cparams(("parallel",)),
        name="merge",
    )(z, a, gates, x, w_oh, w_oa, w_out)


def _mlp_kernel(x_ref, g_ref, wu_ref, wd_ref, o_ref, xn_ref):
    @pl.when(pl.program_id(1) == 0)
    def _():
        x = x_ref[...]
        ms = jnp.mean(x * x, axis=-1, keepdims=True)
        xn_ref[...] = (x * lax.rsqrt(ms + NORM_EPS) * g_ref[...]).astype(BF16)
        o_ref[...] = x

    h = jnp.dot(xn_ref[...], wu_ref[0], preferred_element_type=F32)
    h = jnp.square(jnp.maximum(h, 0.0)).astype(BF16)
    o_ref[...] += jnp.dot(h, wd_ref[...], preferred_element_type=F32)


def _mlp(x, g, w_up_blocks, w_down, *, tm):
    T, D = x.shape
    n_chunks, _, tf = w_up_blocks.shape
    return pl.pallas_call(
        _mlp_kernel,
        grid=(T // tm, n_chunks),
        in_specs=[
            pl.BlockSpec((tm, D), lambda i, j: (i, 0)),
            pl.BlockSpec((1, D), lambda i, j: (0, 0)),
            pl.BlockSpec((1, D, tf), lambda i, j: (j, 0, 0)),
            pl.BlockSpec((tf, D), lambda i, j: (j, 0)),
        ],
        out_specs=pl.BlockSpec((tm, D), lambda i, j: (i, 0)),
        out_shape=jax.ShapeDtypeStruct((T, D), F32),
        scratch_shapes=[pltpu.VMEM((tm, D), BF16)],
        compiler_params=_cparams(("parallel", "arbitrary")),
        name="mlp",
    )(x, g.reshape(1, D), w_up_blocks, w_down)


MLP_FF_TILE = 512


def _column_blocks(w, tn):
    k, n = w.shape
    tn = min(tn, n)
    return w.reshape(k, n // tn, tn).transpose(1, 0, 2)


def _row_tile(T, pref):
    return pref if T % pref == 0 else T


def _encoder_layer(x, p, khat):
    B, L, D = x.shape
    T = B * L
    C = p["hyena_skip"].shape[1]
    off_q = 2 * D
    off_k = off_q + N_HEADS * HEAD_DIM
    off_v = off_k + N_KV_HEADS * HEAD_DIM
    x2 = x.reshape(T, D)
    tm = _row_tile(L, 1024)
    u = _in_proj_hyena(x2, p["norm_mix_g"], p["w_in_hyena"], p["conv_w"], p["conv_b"], B=B, L=L, C=C, tm=tm, tn=512)
    rest = _norm_matmul(x2, p["norm_mix_g"], p["w_in_rest"], tm=_row_tile(T, 512), tn=512, name="in_proj_rest",
                        out_dtype=BF16)
    z = _hyena_mix(u, khat, p["hyena_skip"])
    att = _attention(rest.reshape(B, L, -1), p["q_norm_g"], p["k_norm_g"], p["attn_sink"],
                     off_q=off_q, off_k=off_k, off_v=off_v)
    x1 = _merge(z.reshape(T, C), att.reshape(T, -1), rest, x2, p["w_o_hyena"], p["w_o_attn"],
                p["w_out"], tm=_row_tile(T, 512), tj=512)
    y = _mlp(x1, p["norm_mlp_g"], p["w_up"], p["w_down"], tm=tm)
    return y.reshape(B, L, D)


def kernel(x_prompt, x_sample, norm_mix_g, w_in, conv_w, conv_b, filt_w1, filt_b1, filt_w2, filt_b2,
           filt_w3, filt_freq, hyena_skip, q_norm_g, k_norm_g, attn_sink, w_o_hyena, w_o_attn, w_out,
           norm_mlp_g, w_up, w_down):
    depth = w_in.shape[0]
    D = x_prompt.shape[-1]
    C = hyena_skip.shape[2]
    off_g = w_in.shape[2] - 2 * D
    off_q = (HYENA_ORDER + 1) * C
    y_prompt, y_sample = x_prompt, x_sample
    for l in range(depth):
        p = dict(
            norm_mix_g=norm_mix_g[l], w_in_hyena=w_in[l, :, :off_q].astype(BF16),
            w_in_rest=jnp.concatenate([w_in[l, :, off_g:], w_in[l, :, off_q:off_g]], axis=1).astype(BF16),
            conv_w=conv_w[l], conv_b=conv_b[l],
            hyena_skip=hyena_skip[l], q_norm_g=q_norm_g[l], k_norm_g=k_norm_g[l], attn_sink=attn_sink[l],
            w_o_hyena=w_o_hyena[l].astype(BF16), w_o_attn=w_o_attn[l].astype(BF16),
            w_out=w_out[l].astype(BF16), norm_mlp_g=norm_mlp_g[l],
            w_up=_column_blocks(w_up[l], MLP_FF_TILE).astype(BF16), w_down=w_down[l].astype(BF16),
        )
        filt = (filt_w1[l], filt_b1[l], filt_w2[l], filt_b2[l], filt_w3[l], filt_freq[l])
        outs = []
        for x in (y_prompt, y_sample):
            khat = _filter_spectra(x.shape[1], C, *filt)
            outs.append(_encoder_layer(x, p, khat))
        y_prompt, y_sample = outs
    return (y_prompt, y_sample)
```

```python
import functools
import math

import numpy as np
import jax
import jax.numpy as jnp
from jax import lax
from jax.experimental import pallas as pl
from jax.experimental.pallas import tpu as pltpu

F32 = jnp.float32
BF16 = jnp.bfloat16

NORM_EPS = 1e-6
NEG_INF = -1e30
LOG2E = 1.4426950408889634

HYENA_ORDER = 2
N_HEADS = 8
N_KV_HEADS = 2
HEAD_REP = N_HEADS // N_KV_HEADS
HEAD_DIM = 128
WINDOW = 128
ATT_Q_BLOCKS = 8
FILT_BANDS = 16
DECAY_TARGET = 1e-2
MAX_DECAY = math.log(DECAY_TARGET) / 0.3
MIN_DECAY = math.log(DECAY_TARGET) / 1.5

FFT_N1 = 128
MXU_LANES = 256
KRON_F32 = 8
KRON = 16
MAX_K2_PER_STEP = 32
K2_UNROLL = 16
VMEM_LIMIT = 56 * 1024 * 1024


def _cparams(sem):
    return pltpu.CompilerParams(dimension_semantics=sem, vmem_limit_bytes=VMEM_LIMIT)


def _rms_rows(x, g):
    ms = jnp.mean(x * x, axis=-1, keepdims=True)
    return (x * lax.rsqrt(ms + NORM_EPS) * g).astype(BF16)


def _norm_matmul_kernel(x_ref, g_ref, w_ref, o_ref, *, tn):
    xn = _rms_rows(x_ref[...], g_ref[...])
    for j in range(w_ref.shape[1] // tn):
        cols = slice(j * tn, (j + 1) * tn)
        o_ref[:, cols] = jnp.dot(xn, w_ref[:, cols], preferred_element_type=F32).astype(o_ref.dtype)


def _norm_matmul(x, g, w, *, tm, tn, name, out_dtype):
    T, D = x.shape
    n_out = w.shape[1]
    return pl.pallas_call(
        functools.partial(_norm_matmul_kernel, tn=tn),
        grid=(T // tm,),
        in_specs=[
            pl.BlockSpec((tm, D), lambda i: (i, 0)),
            pl.BlockSpec((1, D), lambda i: (0, 0)),
            pl.BlockSpec((D, n_out), lambda i: (0, 0)),
        ],
        out_specs=pl.BlockSpec((tm, n_out), lambda i: (i, 0)),
        out_shape=jax.ShapeDtypeStruct((T, n_out), out_dtype),
        compiler_params=_cparams(("parallel",)),
        name=name,
    )(x, g.reshape(1, D), w)


HALO = 16


def _in_proj_hyena_kernel(x_ref, xp_ref, xn_ref, g_ref, w_ref, cw_ref, cb_ref, o_ref, xs_ref, *, blocks_per_seq, tn):
    tm = x_ref.shape[0]
    c = o_ref.shape[-1]
    r = pl.program_id(0) % blocks_per_seq
    g = g_ref[...]
    xs_ref[0:HALO, :] = jnp.where(r > 0, _rms_rows(xp_ref[...], g), jnp.zeros((), BF16))
    xs_ref[HALO:HALO + tm, :] = _rms_rows(x_ref[...], g)
    xs_ref[HALO + tm:, :] = jnp.where(r < blocks_per_seq - 1, _rms_rows(xn_ref[...], g), jnp.zeros((), BF16))
    rows = tm + 2 * HALO
    for j in range(w_ref.shape[1] // tn):
        cols = slice(j * tn, (j + 1) * tn)
        p = jnp.dot(xs_ref[...], w_ref[:, cols], preferred_element_type=F32)
        prev = pltpu.roll(p, 1, axis=0)[HALO:HALO + tm]
        nxt = pltpu.roll(p, rows - 1, axis=0)[HALO:HALO + tm]
        u = prev * cw_ref[0:1, cols] + p[HALO:HALO + tm] * cw_ref[1:2, cols] + nxt * cw_ref[2:3, cols] + cb_ref[:, cols]
        comp, c0 = divmod(j * tn, c)
        o_ref[comp, 0, :, c0:c0 + tn] = u.astype(o_ref.dtype)


def _in_proj_hyena(x, g, w, conv_w, conv_b, *, B, L, C, tm, tn):
    T, D = x.shape
    bps = L // tm
    hpb = tm // HALO
    n_out = w.shape[1]
    kern = functools.partial(_in_proj_hyena_kernel, blocks_per_seq=bps, tn=tn)
    return pl.pallas_call(
        kern,
        grid=(T // tm,),
        in_specs=[
            pl.BlockSpec((tm, D), lambda i: (i, 0)),
            pl.BlockSpec((HALO, D), lambda i: (jnp.maximum(i * hpb - 1, 0), 0)),
            pl.BlockSpec((HALO, D), lambda i: (jnp.minimum((i + 1) * hpb, T // HALO - 1), 0)),
            pl.BlockSpec((1, D), lambda i: (0, 0)),
            pl.BlockSpec((D, n_out), lambda i: (0, 0)),
            pl.BlockSpec((3, n_out), lambda i: (0, 0)),
            pl.BlockSpec((1, n_out), lambda i: (0, 0)),
        ],
        out_specs=pl.BlockSpec((n_out // C, 1, tm, C), lambda i: (0, i // bps, i % bps, 0)),
        out_shape=jax.ShapeDtypeStruct((n_out // C, B, L, C), BF16),
        scratch_shapes=[pltpu.VMEM((tm + 2 * HALO, D), BF16)],
        compiler_params=_cparams(("parallel",)),
        name="in_proj_hyena",
    )(x, x, x, g.reshape(1, D), w, conv_w, conv_b.reshape(1, -1))


def _filter_kernel(pos_ref, t_ref, w1_ref, b1_ref, w2_ref, b2_ref, w3_ref, fr_ref,
                   dl_ref, bw_ref, o_ref, h_ref):
    hi = lax.Precision.HIGHEST

    @pl.when(pl.program_id(1) == 0)
    def _():
        fr = fr_ref[...]
        a = jnp.dot(pos_ref[...], w1_ref[...], precision=hi, preferred_element_type=F32) + b1_ref[...]
        a = jnp.sin(fr * a)
        a = jnp.dot(a, w2_ref[...], precision=hi, preferred_element_type=F32) + b2_ref[...]
        h_ref[...] = jnp.sin(fr * a).astype(BF16)

    h = jnp.dot(h_ref[...], w3_ref[...], preferred_element_type=F32)
    h = h * jnp.exp(-t_ref[...] * dl_ref[...])
    row = lax.broadcasted_iota(jnp.int32, h.shape, 0)
    first = (row == 0) & (pl.program_id(0) == 0)
    o_ref[...] = jnp.where(first & (bw_ref[...] > 0.0), 0.0, h).astype(o_ref.dtype)


def _filter_gen(L, C, w1, b1, w2, b2, w3, freq, *, tl, tc):
    emb = w1.shape[0]
    hid = w1.shape[1]
    n_f = w3.shape[1]
    t = np.linspace(0.0, 1.0, L, dtype=np.float32)[:, None]
    w = (2.0 * math.pi * np.arange(L, dtype=np.float32)[:, None] / L).astype(np.float32)
    f = np.linspace(1e-4, FILT_BANDS - 1, FILT_BANDS, dtype=np.float32)[None, :]
    fw = (f * w).astype(np.float32)
    pos = np.zeros((L, 128), np.float32)
    pos[:, :emb] = np.concatenate([t, np.cos(fw), -np.sin(fw)], axis=-1)
    deltas = np.abs(np.linspace(MIN_DECAY, MAX_DECAY, C, dtype=np.float32))
    dl = np.tile(deltas, n_f // C)[None, :]
    bw = np.tile(np.concatenate([np.zeros(C, np.float32), np.ones(C, np.float32)]), n_f // (2 * C))[None, :]
    w1p = jnp.zeros((128, hid), F32).at[:emb].set(w1)
    return pl.pallas_call(
        _filter_kernel,
        grid=(L // tl, n_f // tc),
        in_specs=[
            pl.BlockSpec((tl, 128), lambda i, j: (i, 0)),
            pl.BlockSpec((tl, 1), lambda i, j: (i, 0)),
            pl.BlockSpec((128, hid), lambda i, j: (0, 0)),
            pl.BlockSpec((1, hid), lambda i, j: (0, 0)),
            pl.BlockSpec((hid, hid), lambda i, j: (0, 0)),
            pl.BlockSpec((1, hid), lambda i, j: (0, 0)),
            pl.BlockSpec((hid, tc), lambda i, j: (0, j)),
            pl.BlockSpec((1, hid), lambda i, j: (0, 0)),
            pl.BlockSpec((1, tc), lambda i, j: (0, j)),
            pl.BlockSpec((1, tc), lambda i, j: (0, j)),
        ],
        out_specs=pl.BlockSpec((tl, tc), lambda i, j: (i, j)),
        out_shape=jax.ShapeDtypeStruct((L, n_f), BF16),
        scratch_shapes=[pltpu.VMEM((tl, hid), BF16)],
        compiler_params=_cparams(("parallel", "arbitrary")),
        name="filter_gen",
    )(jnp.asarray(pos), jnp.asarray(t), w1p, b1.reshape(1, hid), w2, b2.reshape(1, hid), w3.astype(BF16),
      freq.reshape(1, hid), jnp.asarray(dl), jnp.asarray(bw))


@functools.lru_cache(maxsize=None)
def _dft_tables(L):
    N = 2 * L
    N1 = FFT_N1
    N2 = N // N1
    H2 = N2 // 2
    k2 = np.arange(H2, dtype=np.float64)
    n2 = np.arange(H2, dtype=np.float64)
    th = 2.0 * np.pi * np.outer(k2 + 0.5, n2) / N2
    ma = np.stack([np.cos(th), -np.sin(th)], axis=1).reshape(N2, H2)
    ga = ma.T * (2.0 / N)
    mak = np.kron(ma, np.eye(KRON_F32))
    gak = np.kron(ga, np.eye(KRON_F32))
    k1 = np.arange(N1, dtype=np.float64)
    n1 = np.arange(N1, dtype=np.float64)
    kk = k1[None, :, None] * N2 + k2[:, None, None] + 0.5
    ph = 2.0 * np.pi * kk * n1[None, None, :] / N
    c, s = np.cos(ph), np.sin(ph)
    fwd = np.concatenate([np.concatenate([c, s], axis=2), np.concatenate([-s, c], axis=2)], axis=1)
    ct, st = np.swapaxes(c, 1, 2), np.swapaxes(s, 1, 2)
    inv = np.concatenate([np.concatenate([ct, -st], axis=2), np.concatenate([st, ct], axis=2)], axis=1)
    return tuple(a.astype(np.float32) for a in (mak, gak, fwd, inv))


def _conv_plan(L):
    H2 = L // FFT_N1
    long_seq = H2 > MAX_K2_PER_STEP
    kb = min(H2, MAX_K2_PER_STEP // 2 if long_seq else MAX_K2_PER_STEP)
    return H2, kb, MXU_LANES, 2 if long_seq else 1


def _mxu_tables(L):
    H2, kb, _, _ = _conv_plan(L)
    mak, gak, fwd, inv = (jnp.asarray(a).astype(BF16) for a in _dft_tables(L))
    return mak.reshape(H2 // kb, 2 * kb * KRON_F32, H2 * KRON_F32), gak, fwd, inv


def _stage_a(z_ref, ma_ref, a_ref):
    h2, tc = z_ref.shape[1], z_ref.shape[3]
    kf = KRON_F32
    for gp in range(FFT_N1 // KRON):
        rows16 = z_ref[0, :, gp * KRON:(gp + 1) * KRON, :].astype(F32)
        parts = []
        for g in range(KRON // kf):
            slab = rows16[:, g * kf:(g + 1) * kf, :].reshape(h2 * kf, tc).astype(BF16)
            r = jnp.dot(ma_ref[0], slab, preferred_element_type=F32)
            parts.append(r.reshape(-1, kf, tc))
        a_ref[:, gp * KRON:(gp + 1) * KRON, :] = jnp.concatenate(parts, axis=1).astype(BF16)


def _k2_loop(a_ref, f_ref, g_ref, k_ref, dst_ref, kb):
    n1 = FFT_N1
    tc = a_ref.shape[-1]

    def body(i, carry):
        d = a_ref[pl.ds(2 * i, 2)].reshape(2 * n1, tc)
        x = jnp.dot(f_ref[i], d, preferred_element_type=F32)
        xr, xi = x[:n1], x[n1:]
        kr, ki = k_ref[0, i, 0], k_ref[0, i, 1]
        y = jnp.concatenate([xr * kr - xi * ki, xr * ki + xi * kr], axis=0).astype(BF16)
        c = jnp.dot(g_ref[i], y, preferred_element_type=F32).astype(BF16)
        dst_ref[i] = c.reshape(2, n1, tc)
        return carry

    lax.fori_loop(0, kb, body, 0, unroll=K2_UNROLL)


def _stage_a_inv(t_ref, ga_ref, z_ref, gate_ref, skip_ref, o_ref):
    h2, rows, tc = z_ref.shape
    kf = KRON_F32
    for gp in range(rows // KRON):
        sl = slice(gp * KRON, (gp + 1) * KRON)
        t16 = t_ref[:, :, sl, :].astype(F32)
        parts = []
        for g in range(KRON // kf):
            slab = t16[:, :, g * kf:(g + 1) * kf, :].reshape(2 * h2 * kf, tc).astype(BF16)
            parts.append(jnp.dot(ga_ref[...], slab, preferred_element_type=F32).reshape(h2, kf, tc))
        y = jnp.concatenate(parts, axis=1)
        out = gate_ref[:, sl, :].astype(F32) * (y + skip_ref[...] * z_ref[:, sl, :].astype(F32))
        o_ref[:, sl, :] = out.astype(o_ref.dtype)


def _conv_fwd_kernel(z_ref, ma_ref, f_ref, g_ref, k_ref, o_ref, a_ref, *, kb):
    _stage_a(z_ref.at[0], ma_ref, a_ref)
    _k2_loop(a_ref, f_ref, g_ref, k_ref, o_ref.at[0], kb)


def _conv_fused_kernel(z_ref, gate_ref, ma_ref, f_ref, g_ref, k_ref, ga_ref, skip_ref, o_ref, a_ref, c_ref, *, kb):
    _stage_a(z_ref.at[0], ma_ref, a_ref)
    _k2_loop(a_ref, f_ref, g_ref, k_ref, c_ref, kb)
    _stage_a_inv(c_ref, ga_ref, z_ref.at[0, 0], gate_ref.at[0, 0], skip_ref, o_ref.at[0, 0])


def _conv_fused(z5, zcomp, gate5, gcomp, mak, fwd, inv, gak, khat, order, skip_row, out_dtype):
    _, B, H2, N1, C = z5.shape
    _, kb, tc, _ = _conv_plan(H2 * N1)
    kern = functools.partial(_conv_fused_kernel, kb=kb)
    tile = lambda comp: pl.BlockSpec((1, 1, H2, N1, tc), lambda c, b: (comp, b, 0, 0, c))
    return pl.pallas_call(
        kern,
        grid=(C // tc, B),
        in_specs=[
            tile(zcomp),
            tile(gcomp),
            pl.BlockSpec((1, 2 * kb * KRON_F32, H2 * KRON_F32), lambda c, b: (0, 0, 0)),
            pl.BlockSpec((kb, 2 * N1, 2 * N1), lambda c, b: (0, 0, 0)),
            pl.BlockSpec((kb, 2 * N1, 2 * N1), lambda c, b: (0, 0, 0)),
            pl.BlockSpec((1, kb, 2, N1, tc), lambda c, b: (order, 0, 0, 0, c)),
            pl.BlockSpec((H2 * KRON_F32, 2 * H2 * KRON_F32), lambda c, b: (0, 0)),
            pl.BlockSpec((1, tc), lambda c, b: (0, c)),
        ],
        out_specs=tile(0),
        out_shape=jax.ShapeDtypeStruct((1, B, H2, N1, C), out_dtype),
        scratch_shapes=[pltpu.VMEM((2 * kb, N1, tc), BF16), pltpu.VMEM((kb, 2, N1, tc), BF16)],
        compiler_params=_cparams(("parallel", "parallel")),
        name="conv_fused",
    )(z5, gate5, mak, fwd, inv, khat, gak, skip_row)


def _conv_fwd(z5, comp, mak, fwd, inv, khat, order):
    _, B, H2, N1, C = z5.shape
    _, kb, tc, _ = _conv_plan(H2 * N1)
    kern = functools.partial(_conv_fwd_kernel, kb=kb)
    return pl.pallas_call(
        kern,
        grid=(C // tc, B, H2 // kb),
        in_specs=[
            pl.BlockSpec((1, 1, H2, N1, tc), lambda c, b, k: (comp, b, 0, 0, c)),
            pl.BlockSpec((1, 2 * kb * KRON_F32, H2 * KRON_F32), lambda c, b, k: (k, 0, 0)),
            pl.BlockSpec((kb, 2 * N1, 2 * N1), lambda c, b, k: (k, 0, 0)),
            pl.BlockSpec((kb, 2 * N1, 2 * N1), lambda c, b, k: (k, 0, 0)),
            pl.BlockSpec((1, kb, 2, N1, tc), lambda c, b, k: (order, k, 0, 0, c)),
        ],
        out_specs=pl.BlockSpec((1, kb, 2, N1, tc), lambda c, b, k: (b, k, 0, 0, c)),
        out_shape=jax.ShapeDtypeStruct((B, H2, 2, N1, C), BF16),
        scratch_shapes=[pltpu.VMEM((2 * kb, N1, tc), BF16)],
        compiler_params=_cparams(("parallel", "parallel", "arbitrary")),
        name="conv_fwd",
    )(z5, mak, fwd, inv, khat)


def _conv_inv_kernel(t_ref, ga_ref, z_ref, gate_ref, skip_ref, o_ref):
    _stage_a_inv(t_ref.at[0], ga_ref, z_ref.at[0, 0], gate_ref.at[0, 0], skip_ref, o_ref.at[0, 0])


def _conv_inv(t2, gak, z5, zcomp, gate5, gcomp, skip_row, out_dtype):
    B, H2, _, N1, C = t2.shape
    _, _, tc, ns = _conv_plan(H2 * N1)
    return pl.pallas_call(
        _conv_inv_kernel,
        grid=(C // tc, B, ns),
        in_specs=[
            pl.BlockSpec((1, H2, 2, N1 // ns, tc), lambda c, b, s: (b, 0, 0, s, c)),
            pl.BlockSpec((H2 * KRON_F32, 2 * H2 * KRON_F32), lambda c, b, s: (0, 0)),
            pl.BlockSpec((1, 1, H2, N1 // ns, tc), lambda c, b, s: (zcomp, b, 0, s, c)),
            pl.BlockSpec((1, 1, H2, N1 // ns, tc), lambda c, b, s: (gcomp, b, 0, s, c)),
            pl.BlockSpec((1, tc), lambda c, b, s: (0, c)),
        ],
        out_specs=pl.BlockSpec((1, 1, H2, N1 // ns, tc), lambda c, b, s: (0, b, 0, s, c)),
        out_shape=jax.ShapeDtypeStruct((1, B, H2, N1, C), out_dtype),
        compiler_params=_cparams(("parallel", "parallel", "parallel")),
        name="conv_inv",
    )(t2, gak, z5, gate5, skip_row)


def _filter_fwd_kernel(hf_ref, hb_ref, ma_ref, f_ref, o_ref, af_ref, ab_ref, *, kb):
    n1 = FFT_N1
    tc = hf_ref.shape[-1]
    _stage_a(hf_ref, ma_ref, af_ref)
    _stage_a(hb_ref, ma_ref, ab_ref)

    def body(i, carry):
        xf = jnp.dot(f_ref[i], af_ref[pl.ds(2 * i, 2)].reshape(2 * n1, tc), preferred_element_type=F32)
        xb = jnp.dot(f_ref[i], ab_ref[pl.ds(2 * i, 2)].reshape(2 * n1, tc), preferred_element_type=F32)
        o_ref[0, i, 0] = xf[:n1] + xb[:n1]
        o_ref[0, i, 1] = xf[n1:] - xb[n1:]
        return carry

    lax.fori_loop(0, kb, body, 0, unroll=K2_UNROLL)


def _filter_fwd(h4, mak, fwd, C):
    _, H2, N1, n_f = h4.shape
    _, kb, tc, _ = _conv_plan(H2 * N1)
    n_order = n_f // (2 * C)
    cpb = C // tc
    kern = functools.partial(_filter_fwd_kernel, kb=kb)
    return pl.pallas_call(
        kern,
        grid=(n_order, cpb, H2 // kb),
        in_specs=[
            pl.BlockSpec((1, H2, N1, tc), lambda o, c, k: (0, 0, 0, (2 * o) * cpb + c)),
            pl.BlockSpec((1, H2, N1, tc), lambda o, c, k: (0, 0, 0, (2 * o + 1) * cpb + c)),
            pl.BlockSpec((1, 2 * kb * KRON_F32, H2 * KRON_F32), lambda o, c, k: (k, 0, 0)),
            pl.BlockSpec((kb, 2 * N1, 2 * N1), lambda o, c, k: (k, 0, 0)),
        ],
        out_specs=pl.BlockSpec((1, kb, 2, N1, tc), lambda o, c, k: (o, k, 0, 0, c)),
        out_shape=jax.ShapeDtypeStruct((n_order, H2, 2, N1, C), F32),
        scratch_shapes=[pltpu.VMEM((2 * kb, N1, tc), BF16), pltpu.VMEM((2 * kb, N1, tc), BF16)],
        compiler_params=_cparams(("parallel", "parallel", "arbitrary")),
        name="filter_fwd",
    )(h4, h4, mak, fwd)


def _filter_spectra(L, C, w1, b1, w2, b2, w3, freq):
    N1 = FFT_N1
    H2 = L // N1
    mak, _, fwd, _ = _mxu_tables(L)
    n_f = w3.shape[1]
    h = _filter_gen(L, C, w1, b1, w2, b2, w3, freq, tl=min(1024, L), tc=min(2048, n_f))
    return _filter_fwd(h.reshape(1, H2, N1, n_f), mak, fwd, C)


def _hyena_mix(u, khat, skip):
    _, B, L, C = u.shape
    N1 = FFT_N1
    H2 = L // N1
    mak, gak, fwd, inv = _mxu_tables(L)
    u5 = u.reshape(3, B, H2, N1, C)
    z5, zc = u5, 0
    _, kb, _, _ = _conv_plan(L)
    for o in range(HYENA_ORDER):
        skip_row = skip[o].astype(F32).reshape(1, C)
        if kb == H2:
            z5 = _conv_fused(z5, zc, u5, o + 1, mak, fwd, inv, gak, khat, o, skip_row, BF16)
        else:
            t2 = _conv_fwd(z5, zc, mak, fwd, inv, khat, o)
            z5 = _conv_inv(t2, gak, z5, zc, u5, o + 1, skip_row, BF16)
        zc = 0
    return z5.reshape(B, L, C)


@functools.lru_cache(maxsize=None)
def _alibi_bias():
    q = np.arange(WINDOW)[:, None]
    kpos = np.arange(3 * WINDOW)[None, :] - WINDOW
    dist = np.abs(q - kpos).astype(np.float32)
    slopes = np.exp2(-8.0 * np.arange(1, N_HEADS + 1, dtype=np.float32) / N_HEADS)
    bias = -slopes[:, None, None] * dist[None] * np.float32(LOG2E)
    bias = np.where(dist[None] <= WINDOW, bias, NEG_INF).astype(np.float32)
    return bias.reshape(N_KV_HEADS, HEAD_REP, WINDOW, 3 * WINDOW).transpose(0, 3, 1, 2).reshape(
        N_KV_HEADS, 3 * WINDOW, HEAD_REP * WINDOW).copy()


def _row_sumsq(x):
    sq = x * x
    hi = sq.astype(BF16)
    lo = (sq - hi.astype(F32)).astype(BF16)
    ones = jnp.ones((x.shape[-1], x.shape[-1]), BF16)
    return (jnp.dot(hi, ones, preferred_element_type=F32) + jnp.dot(lo, ones, preferred_element_type=F32))


def _head_norm(x, g):
    return x * lax.rsqrt(_row_sumsq(x) * (1.0 / x.shape[-1]) + NORM_EPS) * g


def _attention_kernel(q_ref, kp_ref, kc_ref, kn_ref, vp_ref, vc_ref, vn_ref, qg_ref, kg_ref,
                      bias_ref, sink_ref, o_ref, *, n_blocks, nq):
    i = pl.program_id(1)
    hd, blk, rep = HEAD_DIM, WINDOW, HEAD_REP
    kk = jnp.concatenate([kp_ref[0], kc_ref[0], kn_ref[0]], axis=0)
    vv = jnp.concatenate([vp_ref[0], vc_ref[0], vn_ref[0]], axis=0)
    krow = lax.broadcasted_iota(jnp.int32, (3 * blk, 1), 0)
    nt = (((1,), (1,)), ((), ()))
    tn = (((0,), (0,)), ((), ()))
    for g in range(N_KV_HEADS):
        kn_all = _head_norm(kk[:, g * hd:(g + 1) * hd].astype(F32), kg_ref[...]).astype(BF16)
        vg_all = vv[:, g * hd:(g + 1) * hd].astype(BF16)
        sink = sink_ref[g]
        for t in range(nq):
            gi = i * nq + t
            in_seq = ((krow >= blk) | (gi > 0)) & ((krow < 2 * blk) | (gi < n_blocks - 1))
            rows = slice(t * blk, (t + 1) * blk)
            kn = kn_all[t * blk:(t + 3) * blk]
            vg = vg_all[t * blk:(t + 3) * blk]
            qs = jnp.concatenate(
                [q_ref[0, rows, (g * rep + r) * hd:(g * rep + r + 1) * hd] for r in range(rep)], axis=0)
            qn = (_head_norm(qs.astype(F32), qg_ref[...]) * (LOG2E * hd ** -0.5)).astype(BF16)
            st = lax.dot_general(kn, qn, nt, preferred_element_type=F32) + bias_ref[g]
            st = jnp.where(in_seq, st, NEG_INF)
            m = jnp.maximum(jnp.max(st, axis=0, keepdims=True), sink)
            p = jnp.exp2(st - m)
            denom = jnp.sum(p, axis=0, keepdims=True) + jnp.exp2(sink - m)
            ot = lax.dot_general(vg, p.astype(BF16), tn, preferred_element_type=F32) / denom
            for r in range(rep):
                o_ref[0, rows, (g * rep + r) * hd:(g * rep + r + 1) * hd] = (
                    ot[:, r * blk:(r + 1) * blk].T.astype(o_ref.dtype))


def _attention(proj, q_g, k_g, sink, *, off_q, off_k, off_v):
    B, L, _ = proj.shape
    blk = WINDOW
    nb = L // blk
    nq = min(ATT_Q_BLOCKS, nb)
    aw = N_HEADS * HEAD_DIM
    kw = N_KV_HEADS * HEAD_DIM
    qb, kcb, vcb = off_q // aw, off_k // kw, off_v // kw
    kern = functools.partial(_attention_kernel, n_blocks=nb, nq=nq)
    prev = lambda c: (lambda b, i: (b, jnp.maximum(i * nq - 1, 0), c))
    cur = lambda c: (lambda b, i: (b, i, c))
    nxt = lambda c: (lambda b, i: (b, jnp.minimum((i + 1) * nq, nb - 1), c))
    return pl.pallas_call(
        kern,
        grid=(B, nb // nq),
        in_specs=[
            pl.BlockSpec((1, nq * blk, aw), cur(qb)),
            pl.BlockSpec((1, blk, kw), prev(kcb)),
            pl.BlockSpec((1, nq * blk, kw), cur(kcb)),
            pl.BlockSpec((1, blk, kw), nxt(kcb)),
            pl.BlockSpec((1, blk, kw), prev(vcb)),
            pl.BlockSpec((1, nq * blk, kw), cur(vcb)),
            pl.BlockSpec((1, blk, kw), nxt(vcb)),
            pl.BlockSpec((1, HEAD_DIM), lambda b, i: (0, 0)),
            pl.BlockSpec((1, HEAD_DIM), lambda b, i: (0, 0)),
            pl.BlockSpec((N_KV_HEADS, 3 * blk, HEAD_REP * blk), lambda b, i: (0, 0, 0)),
            pl.BlockSpec((N_KV_HEADS, 1, HEAD_REP * blk), lambda b, i: (0, 0, 0)),
        ],
        out_specs=pl.BlockSpec((1, nq * blk, aw), lambda b, i: (b, i, 0)),
        out_shape=jax.ShapeDtypeStruct((B, L, aw), BF16),
        compiler_params=_cparams(("parallel", "parallel")),
        name="attention",
    )(proj, proj, proj, proj, proj, proj, proj, q_g.reshape(1, -1), k_g.reshape(1, -1),
      _alibi_bias(), jnp.repeat(sink.astype(F32) * LOG2E, blk).reshape(N_KV_HEADS, 1, HEAD_REP * blk))


def _merge_kernel(z_ref, a_ref, g_ref, x_ref, woh_ref, woa_ref, wout_ref, o_ref, *, tj):
    d = x_ref.shape[1]
    z = z_ref[...]
    a = a_ref[...]
    acc = x_ref[...]
    for j in range(d // tj):
        cols = slice(j * tj, (j + 1) * tj)
        y_hy = jnp.dot(z, woh_ref[:, cols], preferred_element_type=F32)
        y_at = jnp.dot(a, woa_ref[:, cols], preferred_element_type=F32)
        g_hy = jax.nn.sigmoid(g_ref[:, cols].astype(F32))
        g_at = jax.nn.sigmoid(g_ref[:, d + j * tj:d + (j + 1) * tj].astype(F32))
        mixed = (g_hy * y_hy + g_at * y_at).astype(BF16)
        acc = acc + jnp.dot(mixed, wout_ref[cols, :], preferred_element_type=F32)
    o_ref[...] = acc


def _merge(z, a, gates, x, w_oh, w_oa, w_out, *, tm, tj):
    T, D = x.shape
    cz, ca = z.shape[1], a.shape[1]
    kern = functools.partial(_merge_kernel, tj=tj)
    return pl.pallas_call(
        kern,
        grid=(T // tm,),
        in_specs=[
            pl.BlockSpec((tm, cz), lambda i: (i, 0)),
            pl.BlockSpec((tm, ca), lambda i: (i, 0)),
            pl.BlockSpec((tm, 2 * D), lambda i: (i, 0)),
            pl.BlockSpec((tm, D), lambda i: (i, 0)),
            pl.BlockSpec((cz, D), lambda i: (0, 0)),
            pl.BlockSpec((ca, D), lambda i: (0, 0)),
            pl.BlockSpec((D, D), lambda i: (0, 0)),
        ],
        out_specs=pl.BlockSpec((tm, D), lambda i: (i, 0)),
        out_shape=jax.ShapeDtypeStruct((T, D), F32),
        compiler_params=_cparams(("parallel",)),
        name="merge",
    )(z, a, gates, x, w_oh, w_oa, w_out)


def _mlp_kernel(x_ref, g_ref, wu_ref, wd_ref, o_ref, xn_ref):
    @pl.when(pl.program_id(1) == 0)
    def _():
        x = x_ref[...]
        ms = jnp.mean(x * x, axis=-1, keepdims=True)
        xn_ref[...] = (x * lax.rsqrt(ms + NORM_EPS) * g_ref[...]).astype(BF16)
        o_ref[...] = x

    h = jnp.dot(xn_ref[...], wu_ref[...], preferred_element_type=F32)
    h = jnp.square(jnp.maximum(h, 0.0)).astype(BF16)
    o_ref[...] += jnp.dot(h, wd_ref[...], preferred_element_type=F32)


def _mlp(x, g, w_up, w_down, *, tm, tf):
    T, D = x.shape
    d_ff = w_up.shape[1]
    return pl.pallas_call(
        _mlp_kernel,
        grid=(T // tm, d_ff // tf),
        in_specs=[
            pl.BlockSpec((tm, D), lambda i, j: (i, 0)),
            pl.BlockSpec((1, D), lambda i, j: (0, 0)),
            pl.BlockSpec((D, tf), lambda i, j: (0, j)),
            pl.BlockSpec((tf, D), lambda i, j: (j, 0)),
        ],
        out_specs=pl.BlockSpec((tm, D), lambda i, j: (i, 0)),
        out_shape=jax.ShapeDtypeStruct((T, D), F32),
        scratch_shapes=[pltpu.VMEM((tm, D), BF16)],
        compiler_params=_cparams(("parallel", "arbitrary")),
        name="mlp",
    )(x, g.reshape(1, D), w_up, w_down)


def _row_tile(T, pref):
    return pref if T % pref == 0 else T


def _encoder_layer(x, p, khat):
    B, L, D = x.shape
    T = B * L
    C = p["hyena_skip"].shape[1]
    off_q = 2 * D
    off_k = off_q + N_HEADS * HEAD_DIM
    off_v = off_k + N_KV_HEADS * HEAD_DIM
    x2 = x.reshape(T, D)
    tm = _row_tile(L, 1024)
    u = _in_proj_hyena(x2, p["norm_mix_g"], p["w_in_hyena"], p["conv_w"], p["conv_b"], B=B, L=L, C=C, tm=tm, tn=512)
    rest = _norm_matmul(x2, p["norm_mix_g"], p["w_in_rest"], tm=_row_tile(T, 512), tn=512, name="in_proj_rest",
                        out_dtype=BF16)
    z = _hyena_mix(u, khat, p["hyena_skip"])
    att = _attention(rest.reshape(B, L, -1), p["q_norm_g"], p["k_norm_g"], p["attn_sink"],
                     off_q=off_q, off_k=off_k, off_v=off_v)
    x1 = _merge(z.reshape(T, C), att.reshape(T, -1), rest, x2, p["w_o_hyena"], p["w_o_attn"],
                p["w_out"], tm=_row_tile(T, 512), tj=512)
    y = _mlp(x1, p["norm_mlp_g"], p["w_up"], p["w_down"], tm=tm, tf=512)
    return y.reshape(B, L, D)


def kernel(x_prompt, x_sample, norm_mix_g, w_in, conv_w, conv_b, filt_w1, filt_b1, filt_w2, filt_b2,
           filt_w3, filt_freq, hyena_skip, q_norm_g, k_norm_g, attn_sink, w_o_hyena, w_o_attn, w_out,
           norm_mlp_g, w_up, w_down):
    depth = w_in.shape[0]
    D = x_prompt.shape[-1]
    C = hyena_skip.shape[2]
    off_g = w_in.shape[2] - 2 * D
    off_q = (HYENA_ORDER + 1) * C
    y_prompt, y_sample = x_prompt, x_sample
    for l in range(depth):
        p = dict(
            norm_mix_g=norm_mix_g[l], w_in_hyena=w_in[l, :, :off_q].astype(BF16),
            w_in_rest=jnp.concatenate([w_in[l, :, off_g:], w_in[l, :, off_q:off_g]], axis=1).astype(BF16),
            conv_w=conv_w[l], conv_b=conv_b[l],
            hyena_skip=hyena_skip[l], q_norm_g=q_norm_g[l], k_norm_g=k_norm_g[l], attn_sink=attn_sink[l],
            w_o_hyena=w_o_hyena[l].astype(BF16), w_o_attn=w_o_attn[l].astype(BF16),
            w_out=w_out[l].astype(BF16), norm_mlp_g=norm_mlp_g[l],
            w_up=w_up[l].astype(BF16), w_down=w_down[l].astype(BF16),
        )
        filt = (filt_w1[l], filt_b1[l], filt_w2[l], filt_b2[l], filt_w3[l], filt_freq[l])
        outs = []
        for x in (y_prompt, y_sample):
            khat = _filter_spectra(x.shape[1], C, *filt)
            outs.append(_encoder_layer(x, p, khat))
        y_prompt, y_sample = outs
    return (y_prompt, y_sample)
```

```python
import functools
import math

import numpy as np
import jax
import jax.numpy as jnp
from jax import lax
from jax.experimental import pallas as pl
from jax.experimental.pallas import tpu as pltpu

F32 = jnp.float32
BF16 = jnp.bfloat16

NORM_EPS = 1e-6
NEG_INF = -1e30
LOG2E = 1.4426950408889634

HYENA_ORDER = 2
N_HEADS = 8
N_KV_HEADS = 2
HEAD_REP = N_HEADS // N_KV_HEADS
HEAD_DIM = 128
WINDOW = 128
ATT_Q_BLOCKS = 8
FILT_BANDS = 16
DECAY_TARGET = 1e-2
MAX_DECAY = math.log(DECAY_TARGET) / 0.3
MIN_DECAY = math.log(DECAY_TARGET) / 1.5

FFT_N1 = 128
MXU_LANES = 256
KRON_F32 = 8
KRON = 16
MAX_K2_PER_STEP = 32
K2_UNROLL = 32
VMEM_LIMIT = 56 * 1024 * 1024


def _cparams(sem):
    return pltpu.CompilerParams(dimension_semantics=sem, vmem_limit_bytes=VMEM_LIMIT)


def _rms_rows(x, g):
    ms = jnp.mean(x * x, axis=-1, keepdims=True)
    return (x * lax.rsqrt(ms + NORM_EPS) * g).astype(BF16)


def _norm_matmul_kernel(x_ref, g_ref, w_ref, o_ref, *, tn):
    xn = _rms_rows(x_ref[...], g_ref[...])
    for j in range(w_ref.shape[1] // tn):
        cols = slice(j * tn, (j + 1) * tn)
        o_ref[:, cols] = jnp.dot(xn, w_ref[:, cols], preferred_element_type=F32).astype(o_ref.dtype)


def _norm_matmul(x, g, w, *, tm, tn, name, out_dtype):
    T, D = x.shape
    n_out = w.shape[1]
    return pl.pallas_call(
        functools.partial(_norm_matmul_kernel, tn=tn),
        grid=(T // tm,),
        in_specs=[
            pl.BlockSpec((tm, D), lambda i: (i, 0)),
            pl.BlockSpec((1, D), lambda i: (0, 0)),
            pl.BlockSpec((D, n_out), lambda i: (0, 0)),
        ],
        out_specs=pl.BlockSpec((tm, n_out), lambda i: (i, 0)),
        out_shape=jax.ShapeDtypeStruct((T, n_out), out_dtype),
        compiler_params=_cparams(("parallel",)),
        name=name,
    )(x, g.reshape(1, D), w)


HALO = 16


def _in_proj_hyena_kernel(x_ref, xp_ref, xn_ref, g_ref, w_ref, cw_ref, cb_ref, o_ref, xs_ref, p_ref, *, blocks_per_seq, tn):
    tm = x_ref.shape[0]
    c = o_ref.shape[-1]
    r = pl.program_id(0) % blocks_per_seq
    g = g_ref[...]
    xs_ref[0:HALO, :] = jnp.where(r > 0, _rms_rows(xp_ref[...], g), jnp.zeros((), BF16))
    xs_ref[HALO:HALO + tm, :] = _rms_rows(x_ref[...], g)
    xs_ref[HALO + tm:, :] = jnp.where(r < blocks_per_seq - 1, _rms_rows(xn_ref[...], g), jnp.zeros((), BF16))
    for j in range(w_ref.shape[1] // tn):
        cols = slice(j * tn, (j + 1) * tn)
        p_ref[...] = jnp.dot(xs_ref[...], w_ref[:, cols], preferred_element_type=F32)
        prev = p_ref[pl.ds(HALO - 1, tm), :]
        nxt = p_ref[pl.ds(HALO + 1, tm), :]
        u = prev * cw_ref[0:1, cols] + p_ref[pl.ds(HALO, tm), :] * cw_ref[1:2, cols] + nxt * cw_ref[2:3, cols] + cb_ref[:, cols]
        comp, c0 = divmod(j * tn, c)
        o_ref[comp, 0, :, c0:c0 + tn] = u.astype(o_ref.dtype)


def _in_proj_hyena(x, g, w, conv_w, conv_b, *, B, L, C, tm, tn):
    T, D = x.shape
    bps = L // tm
    hpb = tm // HALO
    n_out = w.shape[1]
    kern = functools.partial(_in_proj_hyena_kernel, blocks_per_seq=bps, tn=tn)
    return pl.pallas_call(
        kern,
        grid=(T // tm,),
        in_specs=[
            pl.BlockSpec((tm, D), lambda i: (i, 0)),
            pl.BlockSpec((HALO, D), lambda i: (jnp.maximum(i * hpb - 1, 0), 0)),
            pl.BlockSpec((HALO, D), lambda i: (jnp.minimum((i + 1) * hpb, T // HALO - 1), 0)),
            pl.BlockSpec((1, D), lambda i: (0, 0)),
            pl.BlockSpec((D, n_out), lambda i: (0, 0)),
            pl.BlockSpec((3, n_out), lambda i: (0, 0)),
            pl.BlockSpec((1, n_out), lambda i: (0, 0)),
        ],
        out_specs=pl.BlockSpec((n_out // C, 1, tm, C), lambda i: (0, i // bps, i % bps, 0)),
        out_shape=jax.ShapeDtypeStruct((n_out // C, B, L, C), BF16),
        scratch_shapes=[pltpu.VMEM((tm + 2 * HALO, D), BF16), pltpu.VMEM((tm + 2 * HALO, tn), F32)],
        compiler_params=_cparams(("parallel",)),
        name="in_proj_hyena",
    )(x, x, x, g.reshape(1, D), w, conv_w, conv_b.reshape(1, -1))


def _filter_kernel(pos_ref, t_ref, w1_ref, b1_ref, w2_ref, b2_ref, w3_ref, fr_ref,
                   dl_ref, bw_ref, o_ref, h_ref):
    hi = lax.Precision.HIGHEST

    @pl.when(pl.program_id(1) == 0)
    def _():
        fr = fr_ref[...]
        a = jnp.dot(pos_ref[...], w1_ref[...], precision=hi, preferred_element_type=F32) + b1_ref[...]
        a = jnp.sin(fr * a)
        a = jnp.dot(a, w2_ref[...], precision=hi, preferred_element_type=F32) + b2_ref[...]
        h_ref[...] = jnp.sin(fr * a).astype(BF16)

    h = jnp.dot(h_ref[...], w3_ref[...], preferred_element_type=F32)
    h = h * jnp.exp(-t_ref[...] * dl_ref[...])
    row = lax.broadcasted_iota(jnp.int32, h.shape, 0)
    first = (row == 0) & (pl.program_id(0) == 0)
    o_ref[...] = jnp.where(first & (bw_ref[...] > 0.0), 0.0, h).astype(o_ref.dtype)


def _filter_gen(L, C, w1, b1, w2, b2, w3, freq, *, tl, tc):
    emb = w1.shape[0]
    hid = w1.shape[1]
    n_f = w3.shape[1]
    t = np.linspace(0.0, 1.0, L, dtype=np.float32)[:, None]
    w = (2.0 * math.pi * np.arange(L, dtype=np.float32)[:, None] / L).astype(np.float32)
    f = np.linspace(1e-4, FILT_BANDS - 1, FILT_BANDS, dtype=np.float32)[None, :]
    fw = (f * w).astype(np.float32)
    pos = np.zeros((L, 128), np.float32)
    pos[:, :emb] = np.concatenate([t, np.cos(fw), -np.sin(fw)], axis=-1)
    deltas = np.abs(np.linspace(MIN_DECAY, MAX_DECAY, C, dtype=np.float32))
    dl = np.tile(deltas, n_f // C)[None, :]
    bw = np.tile(np.concatenate([np.zeros(C, np.float32), np.ones(C, np.float32)]), n_f // (2 * C))[None, :]
    w1p = jnp.zeros((128, hid), F32).at[:emb].set(w1)
    return pl.pallas_call(
        _filter_kernel,
        grid=(L // tl, n_f // tc),
        in_specs=[
            pl.BlockSpec((tl, 128), lambda i, j: (i, 0)),
            pl.BlockSpec((tl, 1), lambda i, j: (i, 0)),
            pl.BlockSpec((128, hid), lambda i, j: (0, 0)),
            pl.BlockSpec((1, hid), lambda i, j: (0, 0)),
            pl.BlockSpec((hid, hid), lambda i, j: (0, 0)),
            pl.BlockSpec((1, hid), lambda i, j: (0, 0)),
            pl.BlockSpec((hid, tc), lambda i, j: (0, j)),
            pl.BlockSpec((1, hid), lambda i, j: (0, 0)),
            pl.BlockSpec((1, tc), lambda i, j: (0, j)),
            pl.BlockSpec((1, tc), lambda i, j: (0, j)),
        ],
        out_specs=pl.BlockSpec((tl, tc), lambda i, j: (i, j)),
        out_shape=jax.ShapeDtypeStruct((L, n_f), BF16),
        scratch_shapes=[pltpu.VMEM((tl, hid), BF16)],
        compiler_params=_cparams(("parallel", "arbitrary")),
        name="filter_gen",
    )(jnp.asarray(pos), jnp.asarray(t), w1p, b1.reshape(1, hid), w2, b2.reshape(1, hid), w3.astype(BF16),
      freq.reshape(1, hid), jnp.asarray(dl), jnp.asarray(bw))


@functools.lru_cache(maxsize=None)
def _dft_tables(L):
    N = 2 * L
    N1 = FFT_N1
    N2 = N // N1
    H2 = N2 // 2
    k2 = np.arange(H2, dtype=np.float64)
    n2 = np.arange(H2, dtype=np.float64)
    th = 2.0 * np.pi * np.outer(k2 + 0.5, n2) / N2
    ma = np.stack([np.cos(th), -np.sin(th)], axis=1).reshape(N2, H2)
    ga = ma.T * (2.0 / N)
    mak = np.kron(ma, np.eye(KRON_F32))
    gak = np.kron(ga, np.eye(KRON_F32))
    k1 = np.arange(N1, dtype=np.float64)
    n1 = np.arange(N1, dtype=np.float64)
    kk = k1[None, :, None] * N2 + k2[:, None, None] + 0.5
    ph = 2.0 * np.pi * kk * n1[None, None, :] / N
    c, s = np.cos(ph), np.sin(ph)
    fwd = np.concatenate([np.concatenate([c, s], axis=2), np.concatenate([-s, c], axis=2)], axis=1)
    ct, st = np.swapaxes(c, 1, 2), np.swapaxes(s, 1, 2)
    inv = np.concatenate([np.concatenate([ct, -st], axis=2), np.concatenate([st, ct], axis=2)], axis=1)
    return tuple(a.astype(np.float32) for a in (mak, gak, fwd, inv))


def _conv_plan(L):
    H2 = L // FFT_N1
    long_seq = H2 > MAX_K2_PER_STEP
    kb = min(H2, MAX_K2_PER_STEP // 2 if long_seq else MAX_K2_PER_STEP)
    return H2, kb, MXU_LANES, 2 if long_seq else 1


def _mxu_tables(L):
    H2, kb, _, _ = _conv_plan(L)
    mak, gak, fwd, inv = (jnp.asarray(a).astype(BF16) for a in _dft_tables(L))
    return mak.reshape(H2 // kb, 2 * kb * KRON_F32, H2 * KRON_F32), gak, fwd, inv


def _stage_a(z_ref, ma_ref, a_ref):
    h2, tc = z_ref.shape[1], z_ref.shape[3]
    kf = KRON_F32
    for gp in range(FFT_N1 // KRON):
        rows16 = z_ref[0, :, gp * KRON:(gp + 1) * KRON, :].astype(F32)
        parts = []
        for g in range(KRON // kf):
            slab = rows16[:, g * kf:(g + 1) * kf, :].reshape(h2 * kf, tc).astype(BF16)
            r = jnp.dot(ma_ref[0], slab, preferred_element_type=F32)
            parts.append(r.reshape(-1, kf, tc))
        a_ref[:, gp * KRON:(gp + 1) * KRON, :] = jnp.concatenate(parts, axis=1).astype(BF16)


def _k2_loop(a_ref, f_ref, g_ref, k_ref, dst_ref, kb):
    n1 = FFT_N1
    tc = a_ref.shape[-1]

    def body(i, carry):
        d = a_ref[pl.ds(2 * i, 2)].reshape(2 * n1, tc)
        x = jnp.dot(f_ref[i], d, preferred_element_type=F32)
        xr, xi = x[:n1], x[n1:]
        kr, ki = k_ref[0, i, 0], k_ref[0, i, 1]
        y = jnp.concatenate([xr * kr - xi * ki, xr * ki + xi * kr], axis=0).astype(BF16)
        c = jnp.dot(g_ref[i], y, preferred_element_type=F32).astype(BF16)
        dst_ref[i] = c.reshape(2, n1, tc)
        return carry

    lax.fori_loop(0, kb, body, 0, unroll=K2_UNROLL)


def _stage_a_inv(t_ref, ga_ref, z_ref, gate_ref, skip_ref, o_ref):
    h2, rows, tc = z_ref.shape
    kf = KRON_F32
    for gp in range(rows // KRON):
        sl = slice(gp * KRON, (gp + 1) * KRON)
        t16 = t_ref[:, :, sl, :].astype(F32)
        parts = []
        for g in range(KRON // kf):
            slab = t16[:, :, g * kf:(g + 1) * kf, :].reshape(2 * h2 * kf, tc).astype(BF16)
            parts.append(jnp.dot(ga_ref[...], slab, preferred_element_type=F32).reshape(h2, kf, tc))
        y = jnp.concatenate(parts, axis=1)
        out = gate_ref[:, sl, :].astype(F32) * (y + skip_ref[...] * z_ref[:, sl, :].astype(F32))
        o_ref[:, sl, :] = out.astype(o_ref.dtype)


def _conv_fwd_kernel(z_ref, ma_ref, f_ref, g_ref, k_ref, o_ref, a_ref, *, kb):
    _stage_a(z_ref.at[0], ma_ref, a_ref)
    _k2_loop(a_ref, f_ref, g_ref, k_ref, o_ref.at[0], kb)


def _conv_fused_kernel(z_ref, gate_ref, ma_ref, f_ref, g_ref, k_ref, ga_ref, skip_ref, o_ref, a_ref, c_ref, *, kb):
    _stage_a(z_ref.at[0], ma_ref, a_ref)
    _k2_loop(a_ref, f_ref, g_ref, k_ref, c_ref, kb)
    _stage_a_inv(c_ref, ga_ref, z_ref.at[0, 0], gate_ref.at[0, 0], skip_ref, o_ref.at[0, 0])


def _conv_fused(z5, zcomp, gate5, gcomp, mak, fwd, inv, gak, khat, order, skip_row, out_dtype):
    _, B, H2, N1, C = z5.shape
    _, kb, tc, _ = _conv_plan(H2 * N1)
    kern = functools.partial(_conv_fused_kernel, kb=kb)
    tile = lambda comp: pl.BlockSpec((1, 1, H2, N1, tc), lambda c, b: (comp, b, 0, 0, c))
    return pl.pallas_call(
        kern,
        grid=(C // tc, B),
        in_specs=[
            tile(zcomp),
            tile(gcomp),
            pl.BlockSpec((1, 2 * kb * KRON_F32, H2 * KRON_F32), lambda c, b: (0, 0, 0)),
            pl.BlockSpec((kb, 2 * N1, 2 * N1), lambda c, b: (0, 0, 0)),
            pl.BlockSpec((kb, 2 * N1, 2 * N1), lambda c, b: (0, 0, 0)),
            pl.BlockSpec((1, kb, 2, N1, tc), lambda c, b: (order, 0, 0, 0, c)),
            pl.BlockSpec((H2 * KRON_F32, 2 * H2 * KRON_F32), lambda c, b: (0, 0)),
            pl.BlockSpec((1, tc), lambda c, b: (0, c)),
        ],
        out_specs=tile(0),
        out_shape=jax.ShapeDtypeStruct((1, B, H2, N1, C), out_dtype),
        scratch_shapes=[pltpu.VMEM((2 * kb, N1, tc), BF16), pltpu.VMEM((kb, 2, N1, tc), BF16)],
        compiler_params=_cparams(("parallel", "parallel")),
        name="conv_fused",
    )(z5, gate5, mak, fwd, inv, khat, gak, skip_row)


def _conv_fwd(z5, comp, mak, fwd, inv, khat, order):
    _, B, H2, N1, C = z5.shape
    _, kb, tc, _ = _conv_plan(H2 * N1)
    kern = functools.partial(_conv_fwd_kernel, kb=kb)
    return pl.pallas_call(
        kern,
        grid=(C // tc, B, H2 // kb),
        in_specs=[
            pl.BlockSpec((1, 1, H2, N1, tc), lambda c, b, k: (comp, b, 0, 0, c)),
            pl.BlockSpec((1, 2 * kb * KRON_F32, H2 * KRON_F32), lambda c, b, k: (k, 0, 0)),
            pl.BlockSpec((kb, 2 * N1, 2 * N1), lambda c, b, k: (k, 0, 0)),
            pl.BlockSpec((kb, 2 * N1, 2 * N1), lambda c, b, k: (k, 0, 0)),
            pl.BlockSpec((1, kb, 2, N1, tc), lambda c, b, k: (order, k, 0, 0, c)),
        ],
        out_specs=pl.BlockSpec((1, kb, 2, N1, tc), lambda c, b, k: (b, k, 0, 0, c)),
        out_shape=jax.ShapeDtypeStruct((B, H2, 2, N1, C), BF16),
        scratch_shapes=[pltpu.VMEM((2 * kb, N1, tc), BF16)],
        compiler_params=_cparams(("parallel", "parallel", "arbitrary")),
        name="conv_fwd",
    )(z5, mak, fwd, inv, khat)


def _conv_inv_kernel(t_ref, ga_ref, z_ref, gate_ref, skip_ref, o_ref):
    _stage_a_inv(t_ref.at[0], ga_ref, z_ref.at[0, 0], gate_ref.at[0, 0], skip_ref, o_ref.at[0, 0])


def _conv_inv(t2, gak, z5, zcomp, gate5, gcomp, skip_row, out_dtype):
    B, H2, _, N1, C = t2.shape
    _, _, tc, ns = _conv_plan(H2 * N1)
    return pl.pallas_call(
        _conv_inv_kernel,
        grid=(C // tc, B, ns),
        in_specs=[
            pl.BlockSpec((1, H2, 2, N1 // ns, tc), lambda c, b, s: (b, 0, 0, s, c)),
            pl.BlockSpec((H2 * KRON_F32, 2 * H2 * KRON_F32), lambda c, b, s: (0, 0)),
            pl.BlockSpec((1, 1, H2, N1 // ns, tc), lambda c, b, s: (zcomp, b, 0, s, c)),
            pl.BlockSpec((1, 1, H2, N1 // ns, tc), lambda c, b, s: (gcomp, b, 0, s, c)),
            pl.BlockSpec((1, tc), lambda c, b, s: (0, c)),
        ],
        out_specs=pl.BlockSpec((1, 1, H2, N1 // ns, tc), lambda c, b, s: (0, b, 0, s, c)),
        out_shape=jax.ShapeDtypeStruct((1, B, H2, N1, C), out_dtype),
        compiler_params=_cparams(("parallel", "parallel", "parallel")),
        name="conv_inv",
    )(t2, gak, z5, gate5, skip_row)


def _filter_fwd_kernel(hf_ref, hb_ref, ma_ref, f_ref, o_ref, af_ref, ab_ref, *, kb):
    n1 = FFT_N1
    tc = hf_ref.shape[-1]
    _stage_a(hf_ref, ma_ref, af_ref)
    _stage_a(hb_ref, ma_ref, ab_ref)

    def body(i, carry):
        xf = jnp.dot(f_ref[i], af_ref[pl.ds(2 * i, 2)].reshape(2 * n1, tc), preferred_element_type=F32)
        xb = jnp.dot(f_ref[i], ab_ref[pl.ds(2 * i, 2)].reshape(2 * n1, tc), preferred_element_type=F32)
        o_ref[0, i, 0] = xf[:n1] + xb[:n1]
        o_ref[0, i, 1] = xf[n1:] - xb[n1:]
        return carry

    lax.fori_loop(0, kb, body, 0, unroll=K2_UNROLL)


def _filter_fwd(h4, mak, fwd, C):
    _, H2, N1, n_f = h4.shape
    _, kb, tc, _ = _conv_plan(H2 * N1)
    n_order = n_f // (2 * C)
    cpb = C // tc
    kern = functools.partial(_filter_fwd_kernel, kb=kb)
    return pl.pallas_call(
        kern,
        grid=(n_order, cpb, H2 // kb),
        in_specs=[
            pl.BlockSpec((1, H2, N1, tc), lambda o, c, k: (0, 0, 0, (2 * o) * cpb + c)),
            pl.BlockSpec((1, H2, N1, tc), lambda o, c, k: (0, 0, 0, (2 * o + 1) * cpb + c)),
            pl.BlockSpec((1, 2 * kb * KRON_F32, H2 * KRON_F32), lambda o, c, k: (k, 0, 0)),
            pl.BlockSpec((kb, 2 * N1, 2 * N1), lambda o, c, k: (k, 0, 0)),
        ],
        out_specs=pl.BlockSpec((1, kb, 2, N1, tc), lambda o, c, k: (o, k, 0, 0, c)),
        out_shape=jax.ShapeDtypeStruct((n_order, H2, 2, N1, C), F32),
        scratch_shapes=[pltpu.VMEM((2 * kb, N1, tc), BF16), pltpu.VMEM((2 * kb, N1, tc), BF16)],
        compiler_params=_cparams(("parallel", "parallel", "arbitrary")),
        name="filter_fwd",
    )(h4, h4, mak, fwd)


def _filter_spectra(L, C, w1, b1, w2, b2, w3, freq):
    N1 = FFT_N1
    H2 = L // N1
    mak, _, fwd, _ = _mxu_tables(L)
    n_f = w3.shape[1]
    h = _filter_gen(L, C, w1, b1, w2, b2, w3, freq, tl=min(1024, L), tc=min(2048, n_f))
    return _filter_fwd(h.reshape(1, H2, N1, n_f), mak, fwd, C)


def _hyena_mix(u, khat, skip):
    _, B, L, C = u.shape
    N1 = FFT_N1
    H2 = L // N1
    mak, gak, fwd, inv = _mxu_tables(L)
    u5 = u.reshape(3, B, H2, N1, C)
    z5, zc = u5, 0
    _, kb, _, _ = _conv_plan(L)
    for o in range(HYENA_ORDER):
        skip_row = skip[o].astype(F32).reshape(1, C)
        if kb == H2:
            z5 = _conv_fused(z5, zc, u5, o + 1, mak, fwd, inv, gak, khat, o, skip_row, BF16)
        else:
            t2 = _conv_fwd(z5, zc, mak, fwd, inv, khat, o)
            z5 = _conv_inv(t2, gak, z5, zc, u5, o + 1, skip_row, BF16)
        zc = 0
    return z5.reshape(B, L, C)


@functools.lru_cache(maxsize=None)
def _alibi_bias():
    q = np.arange(WINDOW)[:, None]
    kpos = np.arange(3 * WINDOW)[None, :] - WINDOW
    dist = np.abs(q - kpos).astype(np.float32)
    slopes = np.exp2(-8.0 * np.arange(1, N_HEADS + 1, dtype=np.float32) / N_HEADS)
    bias = -slopes[:, None, None] * dist[None] * np.float32(LOG2E)
    bias = np.where(dist[None] <= WINDOW, bias, NEG_INF).astype(np.float32)
    return bias.reshape(N_KV_HEADS, HEAD_REP, WINDOW, 3 * WINDOW).transpose(0, 3, 1, 2).reshape(
        N_KV_HEADS, 3 * WINDOW, HEAD_REP * WINDOW).copy()


def _row_sumsq(x):
    sq = x * x
    hi = sq.astype(BF16)
    lo = (sq - hi.astype(F32)).astype(BF16)
    ones = jnp.ones((x.shape[-1], x.shape[-1]), BF16)
    return (jnp.dot(hi, ones, preferred_element_type=F32) + jnp.dot(lo, ones, preferred_element_type=F32))


def _head_norm(x, g):
    return x * lax.rsqrt(_row_sumsq(x) * (1.0 / x.shape[-1]) + NORM_EPS) * g


def _attention_kernel(q_ref, kp_ref, kc_ref, kn_ref, vp_ref, vc_ref, vn_ref, qg_ref, kg_ref,
                      bias_ref, sink_ref, o_ref, *, n_blocks, nq):
    i = pl.program_id(1)
    hd, blk, rep = HEAD_DIM, WINDOW, HEAD_REP
    kk = jnp.concatenate([kp_ref[0], kc_ref[0], kn_ref[0]], axis=0)
    vv = jnp.concatenate([vp_ref[0], vc_ref[0], vn_ref[0]], axis=0)
    krow = lax.broadcasted_iota(jnp.int32, (3 * blk, 1), 0)
    nt = (((1,), (1,)), ((), ()))
    tn = (((0,), (0,)), ((), ()))
    for g in range(N_KV_HEADS):
        kn_all = _head_norm(kk[:, g * hd:(g + 1) * hd].astype(F32), kg_ref[...]).astype(BF16)
        vg_all = vv[:, g * hd:(g + 1) * hd].astype(BF16)
        sink = sink_ref[g]
        for t in range(nq):
            gi = i * nq + t
            in_seq = ((krow >= blk) | (gi > 0)) & ((krow < 2 * blk) | (gi < n_blocks - 1))
            rows = slice(t * blk, (t + 1) * blk)
            kn = kn_all[t * blk:(t + 3) * blk]
            vg = vg_all[t * blk:(t + 3) * blk]
            qs = jnp.concatenate(
                [q_ref[0, rows, (g * rep + r) * hd:(g * rep + r + 1) * hd] for r in range(rep)], axis=0)
            qn = (_head_norm(qs.astype(F32), qg_ref[...]) * (LOG2E * hd ** -0.5)).astype(BF16)
            st = lax.dot_general(kn, qn, nt, preferred_element_type=F32) + bias_ref[g]
            st = jnp.where(in_seq, st, NEG_INF)
            m = jnp.maximum(jnp.max(st, axis=0, keepdims=True), sink)
            p = jnp.exp2(st - m)
            denom = jnp.sum(p, axis=0, keepdims=True) + jnp.exp2(sink - m)
            ot = lax.dot_general(vg, p.astype(BF16), tn, preferred_element_type=F32) / denom
            for r in range(rep):
                o_ref[0, rows, (g * rep + r) * hd:(g * rep + r + 1) * hd] = (
                    ot[:, r * blk:(r + 1) * blk].T.astype(o_ref.dtype))


def _attention(proj, q_g, k_g, sink, *, off_q, off_k, off_v):
    B, L, _ = proj.shape
    blk = WINDOW
    nb = L // blk
    nq = min(ATT_Q_BLOCKS, nb)
    aw = N_HEADS * HEAD_DIM
    kw = N_KV_HEADS * HEAD_DIM
    qb, kcb, vcb = off_q // aw, off_k // kw, off_v // kw
    kern = functools.partial(_attention_kernel, n_blocks=nb, nq=nq)
    prev = lambda c: (lambda b, i: (b, jnp.maximum(i * nq - 1, 0), c))
    cur = lambda c: (lambda b, i: (b, i, c))
    nxt = lambda c: (lambda b, i: (b, jnp.minimum((i + 1) * nq, nb - 1), c))
    return pl.pallas_call(
        kern,
        grid=(B, nb // nq),
        in_specs=[
            pl.BlockSpec((1, nq * blk, aw), cur(qb)),
            pl.BlockSpec((1, blk, kw), prev(kcb)),
            pl.BlockSpec((1, nq * blk, kw), cur(kcb)),
            pl.BlockSpec((1, blk, kw), nxt(kcb)),
            pl.BlockSpec((1, blk, kw), prev(vcb)),
            pl.BlockSpec((1, nq * blk, kw), cur(vcb)),
            pl.BlockSpec((1, blk, kw), nxt(vcb)),
            pl.BlockSpec((1, HEAD_DIM), lambda b, i: (0, 0)),
            pl.BlockSpec((1, HEAD_DIM), lambda b, i: (0, 0)),
            pl.BlockSpec((N_KV_HEADS, 3 * blk, HEAD_REP * blk), lambda b, i: (0, 0, 0)),
            pl.BlockSpec((N_KV_HEADS, 1, HEAD_REP * blk), lambda b, i: (0, 0, 0)),
        ],
        out_specs=pl.BlockSpec((1, nq * blk, aw), lambda b, i: (b, i, 0)),
        out_shape=jax.ShapeDtypeStruct((B, L, aw), BF16),
        compiler_params=_cparams(("parallel", "parallel")),
        name="attention",
    )(proj, proj, proj, proj, proj, proj, proj, q_g.reshape(1, -1), k_g.reshape(1, -1),
      _alibi_bias(), jnp.repeat(sink.astype(F32) * LOG2E, blk).reshape(N_KV_HEADS, 1, HEAD_REP * blk))


def _merge_kernel(z_ref, a_ref, g_ref, x_ref, woh_ref, woa_ref, wout_ref, o_ref, *, tj):
    d = x_ref.shape[1]
    z = z_ref[...]
    a = a_ref[...]
    acc = x_ref[...]
    for j in range(d // tj):
        cols = slice(j * tj, (j + 1) * tj)
        y_hy = jnp.dot(z, woh_ref[:, cols], preferred_element_type=F32)
        y_at = jnp.dot(a, woa_ref[:, cols], preferred_element_type=F32)
        g_hy = jax.nn.sigmoid(g_ref[:, cols].astype(F32))
        g_at = jax.nn.sigmoid(g_ref[:, d + j * tj:d + (j + 1) * tj].astype(F32))
        mixed = (g_hy * y_hy + g_at * y_at).astype(BF16)
        acc = acc + jnp.dot(mixed, wout_ref[cols, :], preferred_element_type=F32)
    o_ref[...] = acc


def _merge(z, a, gates, x, w_oh, w_oa, w_out, *, tm, tj):
    T, D = x.shape
    cz, ca = z.shape[1], a.shape[1]
    kern = functools.partial(_merge_kernel, tj=tj)
    return pl.pallas_call(
        kern,
        grid=(T // tm,),
        in_specs=[
            pl.BlockSpec((tm, cz), lambda i: (i, 0)),
            pl.BlockSpec((tm, ca), lambda i: (i, 0)),
            pl.BlockSpec((tm, 2 * D), lambda i: (i, 0)),
            pl.BlockSpec((tm, D), lambda i: (i, 0)),
            pl.BlockSpec((cz, D), lambda i: (0, 0)),
            pl.BlockSpec((ca, D), lambda i: (0, 0)),
            pl.BlockSpec((D, D), lambda i: (0, 0)),
        ],
        out_specs=pl.BlockSpec((tm, D), lambda i: (i, 0)),
        out_shape=jax.ShapeDtypeStruct((T, D), F32),
        compiler_params=_cparams(("parallel",)),
        name="merge",
    )(z, a, gates, x, w_oh, w_oa, w_out)


def _mlp_kernel(x_ref, g_ref, wu_ref, wd_ref, o_ref, xn_ref):
    @pl.when(pl.program_id(1) == 0)
    def _():
        x = x_ref[...]
        ms = jnp.mean(x * x, axis=-1, keepdims=True)
        xn_ref[...] = (x * lax.rsqrt(ms + NORM_EPS) * g_ref[...]).astype(BF16)
        o_ref[...] = x

    h = jnp.dot(xn_ref[...], wu_ref[...], preferred_element_type=F32)
    h = jnp.square(jnp.maximum(h, 0.0)).astype(BF16)
    o_ref[...] += jnp.dot(h, wd_ref[...], preferred_element_type=F32)


def _mlp(x, g, w_up, w_down, *, tm, tf):
    T, D = x.shape
    d_ff = w_up.shape[1]
    return pl.pallas_call(
        _mlp_kernel,
        grid=(T // tm, d_ff // tf),
        in_specs=[
            pl.BlockSpec((tm, D), lambda i, j: (i, 0)),
            pl.BlockSpec((1, D), lambda i, j: (0, 0)),
            pl.BlockSpec((D, tf), lambda i, j: (0, j)),
            pl.BlockSpec((tf, D), lambda i, j: (j, 0)),
        ],
        out_specs=pl.BlockSpec((tm, D), lambda i, j: (i, 0)),
        out_shape=jax.ShapeDtypeStruct((T, D), F32),
        scratch_shapes=[pltpu.VMEM((tm, D), BF16)],
        compiler_params=_cparams(("parallel", "arbitrary")),
        name="mlp",
    )(x, g.reshape(1, D), w_up, w_down)


def _row_tile(T, pref):
    return pref if T % pref == 0 else T


def _encoder_layer(x, p, khat):
    B, L, D = x.shape
    T = B * L
    C = p["hyena_skip"].shape[1]
    off_q = 2 * D
    off_k = off_q + N_HEADS * HEAD_DIM
    off_v = off_k + N_KV_HEADS * HEAD_DIM
    x2 = x.reshape(T, D)
    tm = _row_tile(L, 1024)
    u = _in_proj_hyena(x2, p["norm_mix_g"], p["w_in_hyena"], p["conv_w"], p["conv_b"], B=B, L=L, C=C, tm=tm, tn=512)
    rest = _norm_matmul(x2, p["norm_mix_g"], p["w_in_rest"], tm=_row_tile(T, 512), tn=512, name="in_proj_rest",
                        out_dtype=BF16)
    z = _hyena_mix(u, khat, p["hyena_skip"])
    att = _attention(rest.reshape(B, L, -1), p["q_norm_g"], p["k_norm_g"], p["attn_sink"],
                     off_q=off_q, off_k=off_k, off_v=off_v)
    x1 = _merge(z.reshape(T, C), att.reshape(T, -1), rest, x2, p["w_o_hyena"], p["w_o_attn"],
                p["w_out"], tm=_row_tile(T, 512), tj=512)
    y = _mlp(x1, p["norm_mlp_g"], p["w_up"], p["w_down"], tm=tm, tf=512)
    return y.reshape(B, L, D)


def kernel(x_prompt, x_sample, norm_mix_g, w_in, conv_w, conv_b, filt_w1, filt_b1, filt_w2, filt_b2,
           filt_w3, filt_freq, hyena_skip, q_norm_g, k_norm_g, attn_sink, w_o_hyena, w_o_attn, w_out,
           norm_mlp_g, w_up, w_down):
    depth = w_in.shape[0]
    D = x_prompt.shape[-1]
    C = hyena_skip.shape[2]
    off_g = w_in.shape[2] - 2 * D
    off_q = (HYENA_ORDER + 1) * C
    y_prompt, y_sample = x_prompt, x_sample
    for l in range(depth):
        p = dict(
            norm_mix_g=norm_mix_g[l], w_in_hyena=w_in[l, :, :off_q].astype(BF16),
            w_in_rest=jnp.concatenate([w_in[l, :, off_g:], w_in[l, :, off_q:off_g]], axis=1).astype(BF16),
            conv_w=conv_w[l], conv_b=conv_b[l],
            hyena_skip=hyena_skip[l], q_norm_g=q_norm_g[l], k_norm_g=k_norm_g[l], attn_sink=attn_sink[l],
            w_o_hyena=w_o_hyena[l].astype(BF16), w_o_attn=w_o_attn[l].astype(BF16),
            w_out=w_out[l].astype(BF16), norm_mlp_g=norm_mlp_g[l],
            w_up=w_up[l].astype(BF16), w_down=w_down[l].astype(BF16),
        )
        filt = (filt_w1[l], filt_b1[l], filt_w2[l], filt_b2[l], filt_w3[l], filt_freq[l])
        outs = []
        for x in (y_prompt, y_sample):
            khat = _filter_spectra(x.shape[1], C, *filt)
            outs.append(_encoder_layer(x, p, khat))
        y_prompt, y_sample = outs
    return (y_prompt, y_sample)
```

```python
import functools
import math

import numpy as np
import jax
import jax.numpy as jnp
from jax import lax
from jax.experimental import pallas as pl
from jax.experimental.pallas import tpu as pltpu

F32 = jnp.float32
BF16 = jnp.bfloat16

NORM_EPS = 1e-6
NEG_INF = -1e30
LOG2E = 1.4426950408889634

HYENA_ORDER = 2
N_HEADS = 8
N_KV_HEADS = 2
HEAD_REP = N_HEADS // N_KV_HEADS
HEAD_DIM = 128
WINDOW = 128
ATT_Q_BLOCKS = 8
FILT_BANDS = 16
DECAY_TARGET = 1e-2
MAX_DECAY = math.log(DECAY_TARGET) / 0.3
MIN_DECAY = math.log(DECAY_TARGET) / 1.5

FFT_N1 = 128
MXU_LANES = 256
KRON_F32 = 8
KRON = 16
MAX_K2_PER_STEP = 32
K2_UNROLL = 32
VMEM_LIMIT = 56 * 1024 * 1024


def _cparams(sem):
    return pltpu.CompilerParams(dimension_semantics=sem, vmem_limit_bytes=VMEM_LIMIT)


def _rms_rows(x, g):
    ms = jnp.mean(x * x, axis=-1, keepdims=True)
    return (x * lax.rsqrt(ms + NORM_EPS) * g).astype(BF16)


def _norm_matmul_kernel(x_ref, g_ref, w_ref, o_ref, *, tn):
    xn = _rms_rows(x_ref[...], g_ref[...])
    for j in range(w_ref.shape[1] // tn):
        cols = slice(j * tn, (j + 1) * tn)
        o_ref[:, cols] = jnp.dot(xn, w_ref[:, cols], preferred_element_type=F32).astype(o_ref.dtype)


def _norm_matmul(x, g, w, *, tm, tn, name, out_dtype):
    T, D = x.shape
    n_out = w.shape[1]
    return pl.pallas_call(
        functools.partial(_norm_matmul_kernel, tn=tn),
        grid=(T // tm,),
        in_specs=[
            pl.BlockSpec((tm, D), lambda i: (i, 0)),
            pl.BlockSpec((1, D), lambda i: (0, 0)),
            pl.BlockSpec((D, n_out), lambda i: (0, 0)),
        ],
        out_specs=pl.BlockSpec((tm, n_out), lambda i: (i, 0)),
        out_shape=jax.ShapeDtypeStruct((T, n_out), out_dtype),
        compiler_params=_cparams(("parallel",)),
        name=name,
    )(x, g.reshape(1, D), w)


HALO = 16


def _in_proj_hyena_kernel(x_ref, xp_ref, xn_ref, g_ref, w_ref, cw_ref, cb_ref, o_ref, xs_ref, p_ref, *, blocks_per_seq, tn):
    tm = x_ref.shape[0]
    c = o_ref.shape[-1]
    r = pl.program_id(0) % blocks_per_seq
    g = g_ref[...]
    xs_ref[0:HALO, :] = jnp.where(r > 0, _rms_rows(xp_ref[...], g), jnp.zeros((), BF16))
    xs_ref[HALO:HALO + tm, :] = _rms_rows(x_ref[...], g)
    xs_ref[HALO + tm:, :] = jnp.where(r < blocks_per_seq - 1, _rms_rows(xn_ref[...], g), jnp.zeros((), BF16))
    for j in range(w_ref.shape[1] // tn):
        cols = slice(j * tn, (j + 1) * tn)
        p_ref[...] = jnp.dot(xs_ref[...], w_ref[:, cols], preferred_element_type=F32)
        prev = p_ref[pl.ds(HALO - 1, tm), :]
        nxt = p_ref[pl.ds(HALO + 1, tm), :]
        u = prev * cw_ref[0:1, cols] + p_ref[pl.ds(HALO, tm), :] * cw_ref[1:2, cols] + nxt * cw_ref[2:3, cols] + cb_ref[:, cols]
        comp, c0 = divmod(j * tn, c)
        o_ref[comp, 0, :, c0:c0 + tn] = u.astype(o_ref.dtype)


def _in_proj_hyena(x, g, w, conv_w, conv_b, *, B, L, C, tm, tn):
    T, D = x.shape
    bps = L // tm
    hpb = tm // HALO
    n_out = w.shape[1]
    kern = functools.partial(_in_proj_hyena_kernel, blocks_per_seq=bps, tn=tn)
    return pl.pallas_call(
        kern,
        grid=(T // tm,),
        in_specs=[
            pl.BlockSpec((tm, D), lambda i: (i, 0)),
            pl.BlockSpec((HALO, D), lambda i: (jnp.maximum(i * hpb - 1, 0), 0)),
            pl.BlockSpec((HALO, D), lambda i: (jnp.minimum((i + 1) * hpb, T // HALO - 1), 0)),
            pl.BlockSpec((1, D), lambda i: (0, 0)),
            pl.BlockSpec((D, n_out), lambda i: (0, 0)),
            pl.BlockSpec((3, n_out), lambda i: (0, 0)),
            pl.BlockSpec((1, n_out), lambda i: (0, 0)),
        ],
        out_specs=pl.BlockSpec((n_out // C, 1, tm, C), lambda i: (0, i // bps, i % bps, 0)),
        out_shape=jax.ShapeDtypeStruct((n_out // C, B, L, C), BF16),
        scratch_shapes=[pltpu.VMEM((tm + 2 * HALO, D), BF16), pltpu.VMEM((tm + 2 * HALO, tn), F32)],
        compiler_params=_cparams(("parallel",)),
        name="in_proj_hyena",
    )(x, x, x, g.reshape(1, D), w, conv_w, conv_b.reshape(1, -1))


def _filter_kernel(pos_ref, t_ref, w1_ref, b1_ref, w2_ref, b2_ref, w3_ref, fr_ref,
                   dl_ref, bw_ref, o_ref, h_ref):
    hi = lax.Precision.HIGHEST

    @pl.when(pl.program_id(1) == 0)
    def _():
        fr = fr_ref[...]
        a = jnp.dot(pos_ref[...], w1_ref[...], precision=hi, preferred_element_type=F32) + b1_ref[...]
        a = jnp.sin(fr * a)
        a = jnp.dot(a, w2_ref[...], precision=hi, preferred_element_type=F32) + b2_ref[...]
        h_ref[...] = jnp.sin(fr * a).astype(BF16)

    h = jnp.dot(h_ref[...], w3_ref[...], preferred_element_type=F32)
    h = h * jnp.exp(-t_ref[...] * dl_ref[...])
    row = lax.broadcasted_iota(jnp.int32, h.shape, 0)
    first = (row == 0) & (pl.program_id(0) == 0)
    o_ref[...] = jnp.where(first & (bw_ref[...] > 0.0), 0.0, h).astype(o_ref.dtype)


def _filter_gen(L, C, w1, b1, w2, b2, w3, freq, *, tl, tc):
    emb = w1.shape[0]
    hid = w1.shape[1]
    n_f = w3.shape[1]
    t = np.linspace(0.0, 1.0, L, dtype=np.float32)[:, None]
    w = (2.0 * math.pi * np.arange(L, dtype=np.float32)[:, None] / L).astype(np.float32)
    f = np.linspace(1e-4, FILT_BANDS - 1, FILT_BANDS, dtype=np.float32)[None, :]
    fw = (f * w).astype(np.float32)
    pos = np.zeros((L, 128), np.float32)
    pos[:, :emb] = np.concatenate([t, np.cos(fw), -np.sin(fw)], axis=-1)
    deltas = np.abs(np.linspace(MIN_DECAY, MAX_DECAY, C, dtype=np.float32))
    dl = np.tile(deltas, n_f // C)[None, :]
    bw = np.tile(np.concatenate([np.zeros(C, np.float32), np.ones(C, np.float32)]), n_f // (2 * C))[None, :]
    w1p = jnp.zeros((128, hid), F32).at[:emb].set(w1)
    return pl.pallas_call(
        _filter_kernel,
        grid=(L // tl, n_f // tc),
        in_specs=[
            pl.BlockSpec((tl, 128), lambda i, j: (i, 0)),
            pl.BlockSpec((tl, 1), lambda i, j: (i, 0)),
            pl.BlockSpec((128, hid), lambda i, j: (0, 0)),
            pl.BlockSpec((1, hid), lambda i, j: (0, 0)),
            pl.BlockSpec((hid, hid), lambda i, j: (0, 0)),
            pl.BlockSpec((1, hid), lambda i, j: (0, 0)),
            pl.BlockSpec((hid, tc), lambda i, j: (0, j)),
            pl.BlockSpec((1, hid), lambda i, j: (0, 0)),
            pl.BlockSpec((1, tc), lambda i, j: (0, j)),
            pl.BlockSpec((1, tc), lambda i, j: (0, j)),
        ],
        out_specs=pl.BlockSpec((tl, tc), lambda i, j: (i, j)),
        out_shape=jax.ShapeDtypeStruct((L, n_f), BF16),
        scratch_shapes=[pltpu.VMEM((tl, hid), BF16)],
        compiler_params=_cparams(("parallel", "arbitrary")),
        name="filter_gen",
    )(jnp.asarray(pos), jnp.asarray(t), w1p, b1.reshape(1, hid), w2, b2.reshape(1, hid), w3.astype(BF16),
      freq.reshape(1, hid), jnp.asarray(dl), jnp.asarray(bw))


@functools.lru_cache(maxsize=None)
def _dft_tables(L):
    N = 2 * L
    N1 = FFT_N1
    N2 = N // N1
    H2 = N2 // 2
    k2 = np.arange(H2, dtype=np.float64)
    n2 = np.arange(H2, dtype=np.float64)
    th = 2.0 * np.pi * np.outer(k2 + 0.5, n2) / N2
    ma = np.stack([np.cos(th), -np.sin(th)], axis=1).reshape(N2, H2)
    ga = ma.T * (2.0 / N)
    mak = np.kron(ma, np.eye(KRON_F32))
    gak = np.kron(ga, np.eye(KRON_F32))
    k1 = np.arange(N1, dtype=np.float64)
    n1 = np.arange(N1, dtype=np.float64)
    kk = k1[None, :, None] * N2 + k2[:, None, None] + 0.5
    ph = 2.0 * np.pi * kk * n1[None, None, :] / N
    c, s = np.cos(ph), np.sin(ph)
    fwd = np.concatenate([np.concatenate([c, s], axis=2), np.concatenate([-s, c], axis=2)], axis=1)
    ct, st = np.swapaxes(c, 1, 2), np.swapaxes(s, 1, 2)
    inv = np.concatenate([np.concatenate([ct, -st], axis=2), np.concatenate([st, ct], axis=2)], axis=1)
    return tuple(a.astype(np.float32) for a in (mak, gak, fwd, inv))


def _conv_plan(L):
    H2 = L // FFT_N1
    long_seq = H2 > MAX_K2_PER_STEP
    kb = min(H2, MAX_K2_PER_STEP // 2 if long_seq else MAX_K2_PER_STEP)
    return H2, kb, MXU_LANES, 2 if long_seq else 1


def _mxu_tables(L):
    H2, kb, _, _ = _conv_plan(L)
    mak, gak, fwd, inv = (jnp.asarray(a).astype(BF16) for a in _dft_tables(L))
    return mak.reshape(H2 // kb, 2 * kb * KRON_F32, H2 * KRON_F32), gak, fwd, inv


def _stage_a(z_ref, ma_ref, a_ref):
    h2, tc = z_ref.shape[1], z_ref.shape[3]
    kf = KRON_F32
    for gp in range(FFT_N1 // KRON):
        rows16 = z_ref[0, :, gp * KRON:(gp + 1) * KRON, :].astype(F32)
        parts = []
        for g in range(KRON // kf):
            slab = rows16[:, g * kf:(g + 1) * kf, :].reshape(h2 * kf, tc).astype(BF16)
            r = jnp.dot(ma_ref[0], slab, preferred_element_type=F32)
            parts.append(r.reshape(-1, kf, tc))
        a_ref[:, gp * KRON:(gp + 1) * KRON, :] = jnp.concatenate(parts, axis=1).astype(BF16)


def _k2_loop(a_ref, f_ref, g_ref, k_ref, dst_ref, kb):
    n1 = FFT_N1
    tc = a_ref.shape[-1]

    def body(i, carry):
        d = a_ref[pl.ds(2 * i, 2)].reshape(2 * n1, tc)
        x = jnp.dot(f_ref[i], d, preferred_element_type=F32)
        xr, xi = x[:n1], x[n1:]
        kr, ki = k_ref[0, i, 0], k_ref[0, i, 1]
        y = jnp.concatenate([xr * kr - xi * ki, xr * ki + xi * kr], axis=0).astype(BF16)
        c = jnp.dot(g_ref[i], y, preferred_element_type=F32).astype(BF16)
        dst_ref[i] = c.reshape(2, n1, tc)
        return carry

    lax.fori_loop(0, kb, body, 0, unroll=K2_UNROLL)


def _stage_a_inv(t_ref, ga_ref, z_ref, gate_ref, skip_ref, o_ref):
    h2, rows, tc = z_ref.shape
    kf = KRON_F32
    for gp in range(rows // KRON):
        sl = slice(gp * KRON, (gp + 1) * KRON)
        t16 = t_ref[:, :, sl, :].astype(F32)
        parts = []
        for g in range(KRON // kf):
            slab = t16[:, :, g * kf:(g + 1) * kf, :].reshape(2 * h2 * kf, tc).astype(BF16)
            parts.append(jnp.dot(ga_ref[...], slab, preferred_element_type=F32).reshape(h2, kf, tc))
        y = jnp.concatenate(parts, axis=1)
        out = gate_ref[:, sl, :].astype(F32) * (y + skip_ref[...] * z_ref[:, sl, :].astype(F32))
        o_ref[:, sl, :] = out.astype(o_ref.dtype)


def _conv_fwd_kernel(z_ref, ma_ref, f_ref, g_ref, k_ref, o_ref, a_ref, *, kb):
    _stage_a(z_ref.at[0], ma_ref, a_ref)
    _k2_loop(a_ref, f_ref, g_ref, k_ref, o_ref.at[0], kb)


def _conv_fused_kernel(z_ref, gate_ref, ma_ref, f_ref, g_ref, k_ref, ga_ref, skip_ref, o_ref, a_ref, c_ref, *, kb):
    _stage_a(z_ref.at[0], ma_ref, a_ref)
    _k2_loop(a_ref, f_ref, g_ref, k_ref, c_ref, kb)
    _stage_a_inv(c_ref, ga_ref, z_ref.at[0, 0], gate_ref.at[0, 0], skip_ref, o_ref.at[0, 0])


def _conv_fused(z5, zcomp, gate5, gcomp, mak, fwd, inv, gak, khat, order, skip_row, out_dtype):
    _, B, H2, N1, C = z5.shape
    _, kb, tc, _ = _conv_plan(H2 * N1)
    kern = functools.partial(_conv_fused_kernel, kb=kb)
    tile = lambda comp: pl.BlockSpec((1, 1, H2, N1, tc), lambda c, b: (comp, b, 0, 0, c))
    return pl.pallas_call(
        kern,
        grid=(C // tc, B),
        in_specs=[
            tile(zcomp),
            tile(gcomp),
            pl.BlockSpec((1, 2 * kb * KRON_F32, H2 * KRON_F32), lambda c, b: (0, 0, 0)),
            pl.BlockSpec((kb, 2 * N1, 2 * N1), lambda c, b: (0, 0, 0)),
            pl.BlockSpec((kb, 2 * N1, 2 * N1), lambda c, b: (0, 0, 0)),
            pl.BlockSpec((1, kb, 2, N1, tc), lambda c, b: (order, 0, 0, 0, c)),
            pl.BlockSpec((H2 * KRON_F32, 2 * H2 * KRON_F32), lambda c, b: (0, 0)),
            pl.BlockSpec((1, tc), lambda c, b: (0, c)),
        ],
        out_specs=tile(0),
        out_shape=jax.ShapeDtypeStruct((1, B, H2, N1, C), out_dtype),
        scratch_shapes=[pltpu.VMEM((2 * kb, N1, tc), BF16), pltpu.VMEM((kb, 2, N1, tc), BF16)],
        compiler_params=_cparams(("parallel", "parallel")),
        name="conv_fused",
    )(z5, gate5, mak, fwd, inv, khat, gak, skip_row)


def _conv_fwd(z5, comp, mak, fwd, inv, khat, order):
    _, B, H2, N1, C = z5.shape
    _, kb, tc, _ = _conv_plan(H2 * N1)
    kern = functools.partial(_conv_fwd_kernel, kb=kb)
    return pl.pallas_call(
        kern,
        grid=(C // tc, B, H2 // kb),
        in_specs=[
            pl.BlockSpec((1, 1, H2, N1, tc), lambda c, b, k: (comp, b, 0, 0, c)),
            pl.BlockSpec((1, 2 * kb * KRON_F32, H2 * KRON_F32), lambda c, b, k: (k, 0, 0)),
            pl.BlockSpec((kb, 2 * N1, 2 * N1), lambda c, b, k: (k, 0, 0)),
            pl.BlockSpec((kb, 2 * N1, 2 * N1), lambda c, b, k: (k, 0, 0)),
            pl.BlockSpec((1, kb, 2, N1, tc), lambda c, b, k: (order, k, 0, 0, c)),
        ],
        out_specs=pl.BlockSpec((1, kb, 2, N1, tc), lambda c, b, k: (b, k, 0, 0, c)),
        out_shape=jax.ShapeDtypeStruct((B, H2, 2, N1, C), BF16),
        scratch_shapes=[pltpu.VMEM((2 * kb, N1, tc), BF16)],
        compiler_params=_cparams(("parallel", "parallel", "arbitrary")),
        name="conv_fwd",
    )(z5, mak, fwd, inv, khat)


def _conv_inv_kernel(t_ref, ga_ref, z_ref, gate_ref, skip_ref, o_ref):
    _stage_a_inv(t_ref.at[0], ga_ref, z_ref.at[0, 0], gate_ref.at[0, 0], skip_ref, o_ref.at[0, 0])


def _conv_inv(t2, gak, z5, zcomp, gate5, gcomp, skip_row, out_dtype):
    B, H2, _, N1, C = t2.shape
    _, _, tc, ns = _conv_plan(H2 * N1)
    return pl.pallas_call(
        _conv_inv_kernel,
        grid=(C // tc, B, ns),
        in_specs=[
            pl.BlockSpec((1, H2, 2, N1 // ns, tc), lambda c, b, s: (b, 0, 0, s, c)),
            pl.BlockSpec((H2 * KRON_F32, 2 * H2 * KRON_F32), lambda c, b, s: (0, 0)),
            pl.BlockSpec((1, 1, H2, N1 // ns, tc), lambda c, b, s: (zcomp, b, 0, s, c)),
            pl.BlockSpec((1, 1, H2, N1 // ns, tc), lambda c, b, s: (gcomp, b, 0, s, c)),
            pl.BlockSpec((1, tc), lambda c, b, s: (0, c)),
        ],
        out_specs=pl.BlockSpec((1, 1, H2, N1 // ns, tc), lambda c, b, s: (0, b, 0, s, c)),
        out_shape=jax.ShapeDtypeStruct((1, B, H2, N1, C), out_dtype),
        compiler_params=_cparams(("parallel", "parallel", "parallel")),
        name="conv_inv",
    )(t2, gak, z5, gate5, skip_row)


def _filter_fwd_kernel(hf_ref, hb_ref, ma_ref, f_ref, o_ref, af_ref, ab_ref, *, kb):
    n1 = FFT_N1
    tc = hf_ref.shape[-1]
    _stage_a(hf_ref, ma_ref, af_ref)
    _stage_a(hb_ref, ma_ref, ab_ref)

    def body(i, carry):
        xf = jnp.dot(f_ref[i], af_ref[pl.ds(2 * i, 2)].reshape(2 * n1, tc), preferred_element_type=F32)
        xb = jnp.dot(f_ref[i], ab_ref[pl.ds(2 * i, 2)].reshape(2 * n1, tc), preferred_element_type=F32)
        o_ref[0, i, 0] = xf[:n1] + xb[:n1]
        o_ref[0, i, 1] = xf[n1:] - xb[n1:]
        return carry

    lax.fori_loop(0, kb, body, 0, unroll=K2_UNROLL)


def _filter_fwd(h4, mak, fwd, C):
    _, H2, N1, n_f = h4.shape
    _, kb, tc, _ = _conv_plan(H2 * N1)
    n_order = n_f // (2 * C)
    cpb = C // tc
    kern = functools.partial(_filter_fwd_kernel, kb=kb)
    return pl.pallas_call(
        kern,
        grid=(n_order, cpb, H2 // kb),
        in_specs=[
            pl.BlockSpec((1, H2, N1, tc), lambda o, c, k: (0, 0, 0, (2 * o) * cpb + c)),
            pl.BlockSpec((1, H2, N1, tc), lambda o, c, k: (0, 0, 0, (2 * o + 1) * cpb + c)),
            pl.BlockSpec((1, 2 * kb * KRON_F32, H2 * KRON_F32), lambda o, c, k: (k, 0, 0)),
            pl.BlockSpec((kb, 2 * N1, 2 * N1), lambda o, c, k: (k, 0, 0)),
        ],
        out_specs=pl.BlockSpec((1, kb, 2, N1, tc), lambda o, c, k: (o, k, 0, 0, c)),
        out_shape=jax.ShapeDtypeStruct((n_order, H2, 2, N1, C), F32),
        scratch_shapes=[pltpu.VMEM((2 * kb, N1, tc), BF16), pltpu.VMEM((2 * kb, N1, tc), BF16)],
        compiler_params=_cparams(("parallel", "parallel", "arbitrary")),
        name="filter_fwd",
    )(h4, h4, mak, fwd)


def _filter_spectra(L, C, w1, b1, w2, b2, w3, freq):
    N1 = FFT_N1
    H2 = L // N1
    mak, _, fwd, _ = _mxu_tables(L)
    n_f = w3.shape[1]
    h = _filter_gen(L, C, w1, b1, w2, b2, w3, freq, tl=min(1024, L), tc=min(2048, n_f))
    return _filter_fwd(h.reshape(1, H2, N1, n_f), mak, fwd, C)


def _hyena_mix(u, khat, skip):
    _, B, L, C = u.shape
    N1 = FFT_N1
    H2 = L // N1
    mak, gak, fwd, inv = _mxu_tables(L)
    u5 = u.reshape(3, B, H2, N1, C)
    z5, zc = u5, 0
    _, kb, _, _ = _conv_plan(L)
    for o in range(HYENA_ORDER):
        skip_row = skip[o].astype(F32).reshape(1, C)
        if kb == H2:
            z5 = _conv_fused(z5, zc, u5, o + 1, mak, fwd, inv, gak, khat, o, skip_row, BF16)
        else:
            t2 = _conv_fwd(z5, zc, mak, fwd, inv, khat, o)
            z5 = _conv_inv(t2, gak, z5, zc, u5, o + 1, skip_row, BF16)
        zc = 0
    return z5.reshape(B, L, C)


@functools.lru_cache(maxsize=None)
def _alibi_bias():
    q = np.arange(WINDOW)[:, None]
    kpos = np.arange(3 * WINDOW)[None, :] - WINDOW
    dist = np.abs(q - kpos).astype(np.float32)
    slopes = np.exp2(-8.0 * np.arange(1, N_HEADS + 1, dtype=np.float32) / N_HEADS)
    bias = -slopes[:, None, None] * dist[None] * np.float32(LOG2E)
    bias = np.where(dist[None] <= WINDOW, bias, NEG_INF).astype(np.float32)
    return bias.reshape(N_KV_HEADS, HEAD_REP, WINDOW, 3 * WINDOW).transpose(0, 3, 1, 2).reshape(
        N_KV_HEADS, 3 * WINDOW, HEAD_REP * WINDOW).copy()


def _row_sumsq(x):
    sq = x * x
    hi = sq.astype(BF16)
    lo = (sq - hi.astype(F32)).astype(BF16)
    ones = jnp.ones((x.shape[-1], x.shape[-1]), BF16)
    return (jnp.dot(hi, ones, preferred_element_type=F32) + jnp.dot(lo, ones, preferred_element_type=F32))


def _head_norm(x, g):
    return x * lax.rsqrt(_row_sumsq(x) * (1.0 / x.shape[-1]) + NORM_EPS) * g


def _attention_kernel(q_ref, kp_ref, kc_ref, kn_ref, vp_ref, vc_ref, vn_ref, qg_ref, kg_ref,
                      bias_ref, sink_ref, o_ref, *, n_blocks, nq):
    i = pl.program_id(1)
    hd, blk, rep = HEAD_DIM, WINDOW, HEAD_REP
    kk = jnp.concatenate([kp_ref[0], kc_ref[0], kn_ref[0]], axis=0)
    vv = jnp.concatenate([vp_ref[0], vc_ref[0], vn_ref[0]], axis=0)
    krow = lax.broadcasted_iota(jnp.int32, (3 * blk, 1), 0)
    nt = (((1,), (1,)), ((), ()))
    tn = (((0,), (0,)), ((), ()))
    for g in range(N_KV_HEADS):
        kn_all = _head_norm(kk[:, g * hd:(g + 1) * hd].astype(F32), kg_ref[...]).astype(BF16)
        vg_all = vv[:, g * hd:(g + 1) * hd].astype(BF16)
        sink = sink_ref[g]
        for t in range(nq):
            gi = i * nq + t
            in_seq = ((krow >= blk) | (gi > 0)) & ((krow < 2 * blk) | (gi < n_blocks - 1))
            rows = slice(t * blk, (t + 1) * blk)
            kn = kn_all[t * blk:(t + 3) * blk]
            vg = vg_all[t * blk:(t + 3) * blk]
            qs = jnp.concatenate(
                [q_ref[0, rows, (g * rep + r) * hd:(g * rep + r + 1) * hd] for r in range(rep)], axis=0)
            qn = (_head_norm(qs.astype(F32), qg_ref[...]) * (LOG2E * hd ** -0.5)).astype(BF16)
            st = lax.dot_general(kn, qn, nt, preferred_element_type=F32) + bias_ref[g]
            st = jnp.where(in_seq, st, NEG_INF)
            m = jnp.maximum(jnp.max(st, axis=0, keepdims=True), sink)
            p = jnp.exp2(st - m)
            denom = jnp.sum(p, axis=0, keepdims=True) + jnp.exp2(sink - m)
            ot = lax.dot_general(vg, p.astype(BF16), tn, preferred_element_type=F32) / denom
            for r in range(rep):
                o_ref[0, rows, (g * rep + r) * hd:(g * rep + r + 1) * hd] = (
                    ot[:, r * blk:(r + 1) * blk].T.astype(o_ref.dtype))


def _attention(proj, q_g, k_g, sink, *, off_q, off_k, off_v):
    B, L, _ = proj.shape
    blk = WINDOW
    nb = L // blk
    nq = min(ATT_Q_BLOCKS, nb)
    aw = N_HEADS * HEAD_DIM
    kw = N_KV_HEADS * HEAD_DIM
    qb, kcb, vcb = off_q // aw, off_k // kw, off_v // kw
    kern = functools.partial(_attention_kernel, n_blocks=nb, nq=nq)
    prev = lambda c: (lambda b, i: (b, jnp.maximum(i * nq - 1, 0), c))
    cur = lambda c: (lambda b, i: (b, i, c))
    nxt = lambda c: (lambda b, i: (b, jnp.minimum((i + 1) * nq, nb - 1), c))
    return pl.pallas_call(
        kern,
        grid=(B, nb // nq),
        in_specs=[
            pl.BlockSpec((1, nq * blk, aw), cur(qb)),
            pl.BlockSpec((1, blk, kw), prev(kcb)),
            pl.BlockSpec((1, nq * blk, kw), cur(kcb)),
            pl.BlockSpec((1, blk, kw), nxt(kcb)),
            pl.BlockSpec((1, blk, kw), prev(vcb)),
            pl.BlockSpec((1, nq * blk, kw), cur(vcb)),
            pl.BlockSpec((1, blk, kw), nxt(vcb)),
            pl.BlockSpec((1, HEAD_DIM), lambda b, i: (0, 0)),
            pl.BlockSpec((1, HEAD_DIM), lambda b, i: (0, 0)),
            pl.BlockSpec((N_KV_HEADS, 3 * blk, HEAD_REP * blk), lambda b, i: (0, 0, 0)),
            pl.BlockSpec((N_KV_HEADS, 1, HEAD_REP * blk), lambda b, i: (0, 0, 0)),
        ],
        out_specs=pl.BlockSpec((1, nq * blk, aw), lambda b, i: (b, i, 0)),
        out_shape=jax.ShapeDtypeStruct((B, L, aw), BF16),
        compiler_params=_cparams(("parallel", "parallel")),
        name="attention",
    )(proj, proj, proj, proj, proj, proj, proj, q_g.reshape(1, -1), k_g.reshape(1, -1),
      _alibi_bias(), jnp.repeat(sink.astype(F32) * LOG2E, blk).reshape(N_KV_HEADS, 1, HEAD_REP * blk))


def _merge_kernel(z_ref, a_ref, g_ref, x_ref, woh_ref, woa_ref, wout_ref, gn_ref, o_ref, on_ref, *, tj):
    d = x_ref.shape[1]
    z = z_ref[...]
    a = a_ref[...]
    acc = x_ref[...]
    for j in range(d // tj):
        cols = slice(j * tj, (j + 1) * tj)
        y_hy = jnp.dot(z, woh_ref[:, cols], preferred_element_type=F32)
        y_at = jnp.dot(a, woa_ref[:, cols], preferred_element_type=F32)
        g_hy = jax.nn.sigmoid(g_ref[:, cols].astype(F32))
        g_at = jax.nn.sigmoid(g_ref[:, d + j * tj:d + (j + 1) * tj].astype(F32))
        mixed = (g_hy * y_hy + g_at * y_at).astype(BF16)
        acc = acc + jnp.dot(mixed, wout_ref[cols, :], preferred_element_type=F32)
    o_ref[...] = acc
    on_ref[...] = _rms_rows(acc, gn_ref[...])


def _merge(z, a, gates, x, w_oh, w_oa, w_out, g_next, *, tm, tj):
    T, D = x.shape
    cz, ca = z.shape[1], a.shape[1]
    kern = functools.partial(_merge_kernel, tj=tj)
    return pl.pallas_call(
        kern,
        grid=(T // tm,),
        in_specs=[
            pl.BlockSpec((tm, cz), lambda i: (i, 0)),
            pl.BlockSpec((tm, ca), lambda i: (i, 0)),
            pl.BlockSpec((tm, 2 * D), lambda i: (i, 0)),
            pl.BlockSpec((tm, D), lambda i: (i, 0)),
            pl.BlockSpec((cz, D), lambda i: (0, 0)),
            pl.BlockSpec((ca, D), lambda i: (0, 0)),
            pl.BlockSpec((D, D), lambda i: (0, 0)),
            pl.BlockSpec((1, D), lambda i: (0, 0)),
        ],
        out_specs=[pl.BlockSpec((tm, D), lambda i: (i, 0)), pl.BlockSpec((tm, D), lambda i: (i, 0))],
        out_shape=[jax.ShapeDtypeStruct((T, D), F32), jax.ShapeDtypeStruct((T, D), BF16)],
        compiler_params=_cparams(("parallel",)),
        name="merge",
    )(z, a, gates, x, w_oh, w_oa, w_out, g_next.reshape(1, D))


def _mlp_kernel(x_ref, xn_ref, wu_ref, wd_ref, o_ref):
    @pl.when(pl.program_id(1) == 0)
    def _():
        o_ref[...] = x_ref[...]

    h = jnp.dot(xn_ref[...], wu_ref[...], preferred_element_type=F32)
    h = jnp.square(jnp.maximum(h, 0.0)).astype(BF16)
    o_ref[...] += jnp.dot(h, wd_ref[...], preferred_element_type=F32)


def _mlp(x, xn, w_up, w_down, *, tm, tf):
    T, D = x.shape
    d_ff = w_up.shape[1]
    return pl.pallas_call(
        _mlp_kernel,
        grid=(T // tm, d_ff // tf),
        in_specs=[
            pl.BlockSpec((tm, D), lambda i, j: (i, 0)),
            pl.BlockSpec((tm, D), lambda i, j: (i, 0)),
            pl.BlockSpec((D, tf), lambda i, j: (0, j)),
            pl.BlockSpec((tf, D), lambda i, j: (j, 0)),
        ],
        out_specs=pl.BlockSpec((tm, D), lambda i, j: (i, 0)),
        out_shape=jax.ShapeDtypeStruct((T, D), F32),
        compiler_params=_cparams(("parallel", "arbitrary")),
        name="mlp",
    )(x, xn, w_up, w_down)


def _row_tile(T, pref):
    return pref if T % pref == 0 else T


def _encoder_layer(x, p, khat):
    B, L, D = x.shape
    T = B * L
    C = p["hyena_skip"].shape[1]
    off_q = 2 * D
    off_k = off_q + N_HEADS * HEAD_DIM
    off_v = off_k + N_KV_HEADS * HEAD_DIM
    x2 = x.reshape(T, D)
    tm = _row_tile(L, 1024)
    u = _in_proj_hyena(x2, p["norm_mix_g"], p["w_in_hyena"], p["conv_w"], p["conv_b"], B=B, L=L, C=C, tm=tm, tn=512)
    rest = _norm_matmul(x2, p["norm_mix_g"], p["w_in_rest"], tm=_row_tile(T, 512), tn=512, name="in_proj_rest",
                        out_dtype=BF16)
    z = _hyena_mix(u, khat, p["hyena_skip"])
    att = _attention(rest.reshape(B, L, -1), p["q_norm_g"], p["k_norm_g"], p["attn_sink"],
                     off_q=off_q, off_k=off_k, off_v=off_v)
    x1, x1n = _merge(z.reshape(T, C), att.reshape(T, -1), rest, x2, p["w_o_hyena"], p["w_o_attn"],
                     p["w_out"], p["norm_mlp_g"], tm=_row_tile(T, 512), tj=1024)
    y = _mlp(x1, x1n, p["w_up"], p["w_down"], tm=tm, tf=512)
    return y.reshape(B, L, D)


def kernel(x_prompt, x_sample, norm_mix_g, w_in, conv_w, conv_b, filt_w1, filt_b1, filt_w2, filt_b2,
           filt_w3, filt_freq, hyena_skip, q_norm_g, k_norm_g, attn_sink, w_o_hyena, w_o_attn, w_out,
           norm_mlp_g, w_up, w_down):
    depth = w_in.shape[0]
    D = x_prompt.shape[-1]
    C = hyena_skip.shape[2]
    off_g = w_in.shape[2] - 2 * D
    off_q = (HYENA_ORDER + 1) * C
    y_prompt, y_sample = x_prompt, x_sample
    for l in range(depth):
        p = dict(
            norm_mix_g=norm_mix_g[l], w_in_hyena=w_in[l, :, :off_q].astype(BF16),
            w_in_rest=jnp.concatenate([w_in[l, :, off_g:], w_in[l, :, off_q:off_g]], axis=1).astype(BF16),
            conv_w=conv_w[l], conv_b=conv_b[l],
            hyena_skip=hyena_skip[l], q_norm_g=q_norm_g[l], k_norm_g=k_norm_g[l], attn_sink=attn_sink[l],
            w_o_hyena=w_o_hyena[l].astype(BF16), w_o_attn=w_o_attn[l].astype(BF16),
            w_out=w_out[l].astype(BF16), norm_mlp_g=norm_mlp_g[l],
            w_up=w_up[l].astype(BF16), w_down=w_down[l].astype(BF16),
        )
        filt = (filt_w1[l], filt_b1[l], filt_w2[l], filt_b2[l], filt_w3[l], filt_freq[l])
        outs = []
        for x in (y_prompt, y_sample):
            khat = _filter_spectra(x.shape[1], C, *filt)
            outs.append(_encoder_layer(x, p, khat))
        y_prompt, y_sample = outs
    return (y_prompt, y_sample)
```
